```python
import math, functools
import jax, jax.numpy as jnp
from jax import lax
import numpy as np

D_MODEL = 1024
BATCH = 2
SEQ = 8192
DEPTH = 2
DEC_BATCH = 32
DEC_SEQ = 4
PAST_LEN = 8192
PAGE_SIZE = 128

N_HEADS = 8
HEAD_DIM = 64
ATT_W = N_HEADS * HEAD_DIM
IDX_HEADS = 4
IDX_DIM = 64
TOPK_MAX = 256
Q_BLOCK = 128
SSM_W = 512
SSM_GROUP = 16
SSM_GROUPS = SSM_W // SSM_GROUP
SSM_STATE = 64
LRU_W = 512
LRU_BLOCKS = 8
LRU_BLOCK = LRU_W // LRU_BLOCKS
CONV_W = 4
LRU_C = 8.0
N_EXPERTS = 32
TOP_K = 4
D_FF = 1024
SWIGLU_LIMIT = 7.0
SWIGLU_ALPHA = 1.702
MOE_BLOCK = 256
N_BRANCH = 3
LN_EPS = 1e-5
DEEPNORM_ALPHA = (2 * DEPTH) ** 0.25
DEEPNORM_INIT = (8 * DEPTH) ** -0.25
F32 = jnp.float32
SPLITS = (ATT_W, ATT_W, ATT_W, IDX_HEADS * IDX_DIM, IDX_DIM, IDX_HEADS, SSM_W, LRU_W, LRU_W, N_BRANCH * D_MODEL)
N_IN = sum(SPLITS)

kernel_name = 'hybrid_dsa_s5_rglru_moe_step'


def _split_points():
    pts, acc = [], 0
    for s in SPLITS[:-1]:
        acc += s
        pts.append(acc)
    return pts


def _layer_norm(x, g, b):
    xf = x.astype(F32)
    mu = jnp.mean(xf, axis=-1, keepdims=True)
    var = jnp.mean(jnp.square(xf - mu), axis=-1, keepdims=True)
    y = (xf - mu) * lax.rsqrt(var + LN_EPS)
    return (y * g.astype(F32) + b.astype(F32)).astype(x.dtype)


def _row_gather(rows, idx):
    return jax.vmap(lambda r, i: r[i])(rows, idx)


def _dsa_attend(q, qi, wi, q_pos, kidx, gather_kv, topk):
    b, nq = q.shape[0], q.shape[1]
    n_keys = kidx.shape[1]
    dots = jnp.einsum('bqhd,bsd->bqhs', qi.astype(F32), kidx.astype(F32))
    score = jnp.einsum('bqh,bqhs->bqs', wi.astype(F32), jax.nn.relu(dots))
    admissible = jnp.arange(n_keys)[None, :] <= q_pos[:, None]
    score = jnp.where(admissible[None], score, -jnp.inf)
    _, sel = lax.top_k(score, topk)
    valid = sel <= q_pos[None, :, None]
    k_sel, v_sel = gather_kv(sel)
    logits = jnp.einsum('bqhd,bqkhd->bhqk', q.astype(F32), k_sel.astype(F32)) * (HEAD_DIM ** -0.5)
    logits = jnp.where(valid[:, None], logits, -jnp.inf)
    p = jax.nn.softmax(logits, axis=-1)
    out = jnp.einsum('bhqk,bqkhd->bqhd', p, v_sel.astype(F32))
    return out.reshape(b, nq, ATT_W).astype(q.dtype)


def _dsa_prompt(q, k, v, qi, ki, wi):
    b, t = q.shape[0], q.shape[1]
    topk = min(TOPK_MAX, t // 4)

    def gather_kv(sel):
        return _row_gather(k, sel), _row_gather(v, sel)

    def block(i):
        s = i * Q_BLOCK
        sl = lambda a: lax.dynamic_slice_in_dim(a, s, Q_BLOCK, axis=1)
        pos = s + jnp.arange(Q_BLOCK)
        return _dsa_attend(sl(q), sl(qi), sl(wi), pos, ki, gather_kv, topk)

    out = lax.map(block, jnp.arange(t // Q_BLOCK))
    return jnp.swapaxes(out, 0, 1).reshape(b, t, ATT_W)


def _dsa_sample(q, k, v, qi, ki, wi, cache_k, cache_v, cache_kidx, layer, page_table):
    db, ns = q.shape[0], q.shape[1]
    past = page_table.shape[1] * PAGE_SIZE
    topk = min(TOPK_MAX, (past + ns) // 4)
    past_ki = cache_kidx[layer, page_table].reshape(db, past, IDX_DIM)
    kidx = jnp.concatenate([past_ki.astype(ki.dtype), ki], axis=1)
    pos = past + jnp.arange(ns)

    def gather_kv(sel):
        in_past = sel < past
        ps = jnp.minimum(sel, past - 1)
        phys = jnp.take_along_axis(page_table, (ps // PAGE_SIZE).reshape(db, -1), axis=1).reshape(sel.shape)
        off = ps % PAGE_SIZE
        sn = jnp.clip(sel - past, 0, ns - 1)
        m = in_past[..., None, None]
        k_sel = jnp.where(m, cache_k[layer, phys, off].astype(k.dtype), _row_gather(k, sn))
        v_sel = jnp.where(m, cache_v[layer, phys, off].astype(v.dtype), _row_gather(v, sn))
        return k_sel, v_sel

    return _dsa_attend(q, qi, wi, pos, kidx, gather_kv, topk)


def _s5(u, h0_re, h0_im, lam_re, lam_im, log_dt, b_re, b_im, c_re, c_im, d_skip, w_glu, b_glu):
    b, t = u.shape[0], u.shape[1]
    lam = lax.complex(lam_re.astype(F32), lam_im.astype(F32))
    dt = jnp.exp(log_dt.astype(F32))[:, None]
    lam_bar = jnp.exp(lam * dt)
    b_mat = lax.complex(b_re.astype(F32), b_im.astype(F32))
    b_bar = ((lam_bar - 1.0) / lam)[..., None] * b_mat
    ug = u.astype(F32).reshape(b, t, SSM_GROUPS, SSM_GROUP)
    bu = jnp.einsum('gpc,btgc->btgp', b_bar, ug)
    h0 = lax.complex(h0_re.astype(F32), h0_im.astype(F32))
    bu = bu.at[:, 0].add(lam_bar * h0)
    a = jnp.broadcast_to(lam_bar, bu.shape)

    def comb(e1, e2):
        a1, b1 = e1
        a2, b2 = e2
        return a1 * a2, a2 * b1 + b2

    _, h = lax.associative_scan(comb, (a, bu), axis=1)
    y = (jnp.einsum('gcp,btgp->btgc', c_re.astype(F32), h.real)
         - jnp.einsum('gcp,btgp->btgc', c_im.astype(F32), h.imag)
         + d_skip.astype(F32) * ug).reshape(b, t, SSM_W)
    z = jax.nn.gelu(y)
    z = z * jax.nn.sigmoid(z @ w_glu.astype(F32) + b_glu.astype(F32))
    h_last = h[:, -1]
    return z.astype(u.dtype), h_last.real, h_last.imag


def _rglru(xr, x_gate, conv_buf, h0, conv_w, conv_b, w_a, b_a, w_x, b_x, lam):
    b, t = xr.shape[0], xr.shape[1]
    xpad = jnp.concatenate([conv_buf.astype(xr.dtype), xr], axis=1)
    xc = conv_b + sum(conv_w[j] * xpad[:, j:j + t] for j in range(CONV_W))
    new_buf = xpad[:, t:]
    xf = xc.astype(F32)
    xb = xf.reshape(b, t, LRU_BLOCKS, LRU_BLOCK)
    r = jax.nn.sigmoid(jnp.einsum('btnc,ncd->btnd', xb, w_a.astype(F32)).reshape(b, t, LRU_W) + b_a.astype(F32))
    i = jax.nn.sigmoid(jnp.einsum('btnc,ncd->btnd', xb, w_x.astype(F32)).reshape(b, t, LRU_W) + b_x.astype(F32))
    log_a = LRU_C * r * jax.nn.log_sigmoid(lam.astype(F32))
    a = jnp.exp(log_a)
    bx = jnp.sqrt(-jnp.expm1(2.0 * log_a)) * (i * xf)
    bx = bx.at[:, 0].add(a[:, 0] * h0.astype(F32))

    def comb(e1, e2):
        a1, b1 = e1
        a2, b2 = e2
        return a1 * a2, a2 * b1 + b2

    _, h = lax.associative_scan(comb, (a, bx), axis=1)
    y = h * jax.nn.gelu(x_gate.astype(F32))
    return y.astype(xr.dtype), new_buf, h[:, -1]


def _moe(x, w_r, b_r, w_gu, b_gu, w_dn, b_dn):
    shape = x.shape
    t = x.reshape(-1, D_MODEL)
    n = t.shape[0]
    logits = t.astype(F32) @ w_r.astype(F32) + b_r.astype(F32)
    top_v, top_i = lax.top_k(logits, TOP_K)
    comb = jnp.sum(jax.nn.one_hot(top_i, N_EXPERTS, dtype=F32)
                   * jax.nn.softmax(top_v, axis=-1)[..., None], axis=1)
    n_blk = -(-n // MOE_BLOCK)
    pad = n_blk * MOE_BLOCK - n
    t_b = jnp.pad(t, ((0, pad), (0, 0))).reshape(n_blk, MOE_BLOCK, D_MODEL)
    c_b = jnp.pad(comb, ((0, pad), (0, 0))).reshape(n_blk, MOE_BLOCK, N_EXPERTS)

    def block(args):
        tb, cb = args
        gu = jnp.einsum('md,edf->mef', tb, w_gu) + b_gu
        gate = jnp.minimum(gu[..., :D_FF], SWIGLU_LIMIT)
        up = jnp.clip(gu[..., D_FF:], -SWIGLU_LIMIT, SWIGLU_LIMIT)
        act = (up + 1.0) * gate * jax.nn.sigmoid(SWIGLU_ALPHA * gate)
        out_e = jnp.einsum('mef,efd->med', act, w_dn) + b_dn
        return jnp.einsum('me,med->md', cb.astype(out_e.dtype), out_e)

    out = lax.map(block, (t_b, c_b))
    return out.reshape(-1, D_MODEL)[:n].reshape(shape).astype(x.dtype)


def _layer(x, attn_fn, ssm_h0_re, ssm_h0_im, conv_buf, lru_h0, w_in, b_in, ssm_p, lru_p, mix_p, moe_p, ln2_p):
    b, t = x.shape[0], x.shape[1]
    h = x @ w_in + b_in
    q, k, v, qi, ki, wi, u, xr, xg, gl = jnp.split(h, _split_points(), axis=-1)
    q = q.reshape(b, t, N_HEADS, HEAD_DIM)
    k = k.reshape(b, t, N_HEADS, HEAD_DIM)
    v = v.reshape(b, t, N_HEADS, HEAD_DIM)
    qi = qi.reshape(b, t, IDX_HEADS, IDX_DIM)
    y_att = attn_fn(q, k, v, qi, ki, wi)
    y_ssm, s_re, s_im = _s5(u, ssm_h0_re, ssm_h0_im, *ssm_p)
    y_lru, new_buf, h_last = _rglru(xr, xg, conv_buf, lru_h0, *lru_p)
    w_branch, w_out, b_out, ln1_g, ln1_b = mix_p
    br = jnp.stack([y_att, y_ssm, y_lru], axis=2)
    proj = jnp.einsum('btnc,ncd->btnd', br, w_branch)
    gates = jax.nn.sigmoid(gl.reshape(b, t, N_BRANCH, D_MODEL))
    mixed = jnp.sum(gates * proj, axis=2) @ w_out + b_out
    x = _layer_norm(DEEPNORM_ALPHA * x + mixed, ln1_g, ln1_b)
    x = _layer_norm(DEEPNORM_ALPHA * x + _moe(x, *moe_p), *ln2_p)
    return x, (k, v, ki, s_re, s_im, new_buf, h_last)


def setup_inputs(seed: int = 0) -> dict:
    key = jax.random.key(seed)
    ks = iter(jax.random.split(key, 64))
    nrm = lambda shape, scale: jax.random.normal(next(ks), shape, F32) * scale
    n_pages = PAST_LEN // PAGE_SIZE
    n_pool = (5 * DEC_BATCH * n_pages + 3) // 4
    d = DEPTH
    x_prompt = nrm((BATCH, SEQ, D_MODEL), 1.0)
    x_sample = nrm((DEC_BATCH, DEC_SEQ, D_MODEL), 1.0)
    cache_k = nrm((d, n_pool, PAGE_SIZE, N_HEADS, HEAD_DIM), 1.0)
    cache_v = nrm((d, n_pool, PAGE_SIZE, N_HEADS, HEAD_DIM), 1.0)
    cache_kidx = nrm((d, n_pool, PAGE_SIZE, IDX_DIM), 1.0)
    state_ssm_re = nrm((d, DEC_BATCH, SSM_GROUPS, SSM_STATE), 0.1)
    state_ssm_im = nrm((d, DEC_BATCH, SSM_GROUPS, SSM_STATE), 0.1)
    state_lru_h = nrm((d, DEC_BATCH, LRU_W), 0.5)
    state_lru_conv = nrm((d, DEC_BATCH, CONV_W - 1, LRU_W), 1.0)
    page_table = jax.random.permutation(next(ks), n_pool)[:DEC_BATCH * n_pages].reshape(DEC_BATCH, n_pages).astype(jnp.int32)

    w_in = nrm((d, D_MODEL, N_IN), D_MODEL ** -0.5)
    b_in = nrm((d, N_IN), 0.01)
    ssm_lam_re = -0.5 + nrm((d, SSM_GROUPS, SSM_STATE), 0.01)
    ssm_lam_im = math.pi * jnp.arange(SSM_STATE, dtype=F32) + nrm((d, SSM_GROUPS, SSM_STATE), 0.01)
    ssm_log_dt = jax.random.uniform(next(ks), (d, SSM_GROUPS), F32, math.log(1e-3), math.log(1e-1))
    ssm_b_re = nrm((d, SSM_GROUPS, SSM_STATE, SSM_GROUP), (2.0 * SSM_GROUP) ** -0.5)
    ssm_b_im = nrm((d, SSM_GROUPS, SSM_STATE, SSM_GROUP), (2.0 * SSM_GROUP) ** -0.5)
    ssm_c_re = nrm((d, SSM_GROUPS, SSM_GROUP, SSM_STATE), SSM_STATE ** -0.5)
    ssm_c_im = nrm((d, SSM_GROUPS, SSM_GROUP, SSM_STATE), SSM_STATE ** -0.5)
    ssm_d = nrm((d, SSM_GROUPS, SSM_GROUP), 1.0)
    ssm_w_glu = nrm((d, SSM_W, SSM_W), SSM_W ** -0.5)
    ssm_b_glu = nrm((d, SSM_W), 0.01)
    lru_conv_w = nrm((d, CONV_W, LRU_W), CONV_W ** -0.5)
    lru_conv_b = nrm((d, LRU_W), 0.01)
    lru_w_a = nrm((d, LRU_BLOCKS, LRU_BLOCK, LRU_BLOCK), LRU_BLOCK ** -0.5)
    lru_b_a = nrm((d, LRU_W), 0.01)
    lru_w_x = nrm((d, LRU_BLOCKS, LRU_BLOCK, LRU_BLOCK), LRU_BLOCK ** -0.5)
    lru_b_x = nrm((d, LRU_W), 0.01)
    a_init = jax.random.uniform(next(ks), (d, LRU_W), F32, 0.9, 0.999)
    s_init = a_init ** (1.0 / LRU_C)
    lru_lam = jnp.log(s_init) - jnp.log1p(-s_init)
    w_branch = nrm((d, N_BRANCH, ATT_W, D_MODEL), ATT_W ** -0.5)
    w_out = nrm((d, D_MODEL, D_MODEL), DEEPNORM_INIT * D_MODEL ** -0.5)
    b_out = nrm((d, D_MODEL), 0.01)
    ln1_g = 1.0 + nrm((d, D_MODEL), 0.02)
    ln1_b = nrm((d, D_MODEL), 0.01)
    moe_w_r = nrm((d, D_MODEL, N_EXPERTS), D_MODEL ** -0.5)
    moe_b_r = nrm((d, N_EXPERTS), 0.01)
    moe_w_gu = nrm((d, N_EXPERTS, D_MODEL, 2 * D_FF), D_MODEL ** -0.5)
    moe_b_gu = nrm((d, N_EXPERTS, 2 * D_FF), 0.01)
    moe_w_dn = nrm((d, N_EXPERTS, D_FF, D_MODEL), DEEPNORM_INIT * D_FF ** -0.5)
    moe_b_dn = nrm((d, N_EXPERTS, D_MODEL), 0.01)
    ln2_g = 1.0 + nrm((d, D_MODEL), 0.02)
    ln2_b = nrm((d, D_MODEL), 0.01)
    return {'x_prompt': x_prompt, 'x_sample': x_sample, 'cache_k': cache_k, 'cache_v': cache_v,
            'cache_kidx': cache_kidx, 'state_ssm_re': state_ssm_re, 'state_ssm_im': state_ssm_im,
            'state_lru_h': state_lru_h, 'state_lru_conv': state_lru_conv, 'page_table': page_table,
            'w_in': w_in, 'b_in': b_in, 'ssm_lam_re': ssm_lam_re, 'ssm_lam_im': ssm_lam_im,
            'ssm_log_dt': ssm_log_dt, 'ssm_b_re': ssm_b_re, 'ssm_b_im': ssm_b_im, 'ssm_c_re': ssm_c_re,
            'ssm_c_im': ssm_c_im, 'ssm_d': ssm_d, 'ssm_w_glu': ssm_w_glu, 'ssm_b_glu': ssm_b_glu,
            'lru_conv_w': lru_conv_w, 'lru_conv_b': lru_conv_b, 'lru_w_a': lru_w_a, 'lru_b_a': lru_b_a,
            'lru_w_x': lru_w_x, 'lru_b_x': lru_b_x, 'lru_lam': lru_lam, 'w_branch': w_branch,
            'w_out': w_out, 'b_out': b_out, 'ln1_g': ln1_g, 'ln1_b': ln1_b, 'moe_w_r': moe_w_r,
            'moe_b_r': moe_b_r, 'moe_w_gu': moe_w_gu, 'moe_b_gu': moe_b_gu, 'moe_w_dn': moe_w_dn,
            'moe_b_dn': moe_b_dn, 'ln2_g': ln2_g, 'ln2_b': ln2_b}


def reference(x_prompt, x_sample, cache_k, cache_v, cache_kidx, state_ssm_re, state_ssm_im, state_lru_h,
              state_lru_conv, page_table, w_in, b_in, ssm_lam_re, ssm_lam_im, ssm_log_dt, ssm_b_re, ssm_b_im,
              ssm_c_re, ssm_c_im, ssm_d, ssm_w_glu, ssm_b_glu, lru_conv_w, lru_conv_b, lru_w_a, lru_b_a,
              lru_w_x, lru_b_x, lru_lam, w_branch, w_out, b_out, ln1_g, ln1_b, moe_w_r, moe_b_r, moe_w_gu,
              moe_b_gu, moe_w_dn, moe_b_dn, ln2_g, ln2_b):
    xp, xs = x_prompt, x_sample
    bp = xp.shape[0]
    new_p = [[] for _ in range(7)]
    new_s = [[] for _ in range(7)]
    for l in range(DEPTH):
        ssm_p = (ssm_lam_re[l], ssm_lam_im[l], ssm_log_dt[l], ssm_b_re[l], ssm_b_im[l], ssm_c_re[l],
                 ssm_c_im[l], ssm_d[l], ssm_w_glu[l], ssm_b_glu[l])
        lru_p = (lru_conv_w[l], lru_conv_b[l], lru_w_a[l], lru_b_a[l], lru_w_x[l], lru_b_x[l], lru_lam[l])
        mix_p = (w_branch[l], w_out[l], b_out[l], ln1_g[l], ln1_b[l])
        moe_p = (moe_w_r[l], moe_b_r[l], moe_w_gu[l], moe_b_gu[l], moe_w_dn[l], moe_b_dn[l])
        ln2_p = (ln2_g[l], ln2_b[l])
        z_ssm = jnp.zeros((bp, SSM_GROUPS, SSM_STATE), F32)
        xp, st_p = _layer(xp, _dsa_prompt, z_ssm, z_ssm, jnp.zeros((bp, CONV_W - 1, LRU_W), xp.dtype),
                          jnp.zeros((bp, LRU_W), F32), w_in[l], b_in[l], ssm_p, lru_p, mix_p, moe_p, ln2_p)
        attn_s = functools.partial(_dsa_sample, cache_k=cache_k, cache_v=cache_v, cache_kidx=cache_kidx,
                                   layer=l, page_table=page_table)
        xs, st_s = _layer(xs, attn_s, state_ssm_re[l], state_ssm_im[l], state_lru_conv[l], state_lru_h[l],
                          w_in[l], b_in[l], ssm_p, lru_p, mix_p, moe_p, ln2_p)
        for j in range(7):
            new_p[j].append(st_p[j])
            new_s[j].append(st_s[j])
    k_p, v_p, ki_p, sre_p, sim_p, lc_p, lh_p = [jnp.stack(a) for a in new_p]
    k_s, v_s, ki_s, sre_s, sim_s, lc_s, lh_s = [jnp.stack(a) for a in new_s]
    return (xp, xs, k_p, v_p, ki_p, sre_p, sim_p, lh_p, lc_p, k_s, v_s, ki_s, sre_s, sim_s, lh_s, lc_s)
```

```python
import functools
import math

import jax
import jax.numpy as jnp
from jax import lax
from jax.experimental import pallas as pl
from jax.experimental.pallas import tpu as pltpu

F32 = jnp.float32
BF16 = jnp.bfloat16

D_MODEL = 1024
N_HEADS = 8
HEAD_DIM = 64
ATT_W = N_HEADS * HEAD_DIM
IDX_HEADS = 4
IDX_DIM = 64
TOPK_MAX = 256
PAGE_SIZE = 128
SSM_W = 512
SSM_GROUPS = 32
SSM_GROUP = 16
SSM_STATE = 64
SSM_N = SSM_GROUPS * SSM_STATE
LRU_W = 512
CONV_W = 4
LRU_C = 8.0
N_EXPERTS = 32
TOP_K = 4
D_FF = 1024
SWIGLU_LIMIT = 7.0
SWIGLU_ALPHA = 1.702
N_BRANCH = 3
LN_EPS = 1e-5
IDX_PAD = 384
IDX_KI = IDX_HEADS * IDX_DIM
IDX_WI = IDX_KI + IDX_DIM
NEG_BIG = -1e30
SUBLANES = 8
LANES = 128


def _cparams(sem, vmem_mb):
    return pltpu.CompilerParams(dimension_semantics=sem, vmem_limit_bytes=vmem_mb * 1024 * 1024)


_SEG_W = (ATT_W, ATT_W, ATT_W, IDX_PAD, SSM_W, LRU_W, LRU_W, N_BRANCH * D_MODEL)
_SEG_OFF = tuple(sum(_SEG_W[:i]) for i in range(len(_SEG_W)))
N_IN_PAD = sum(_SEG_W)


def _in_proj_kernel(x_ref, w_ref, b_ref, q_ref, k_ref, kb_ref, v_ref, vb_ref, idx_ref, u_ref, xr_ref,
                    xg_ref, gl_ref):
    xb = x_ref[...].astype(BF16)

    def seg(i):
        off, wd = _SEG_OFF[i], _SEG_W[i]
        return jnp.dot(xb, w_ref[:, off:off + wd], preferred_element_type=F32) + b_ref[:, off:off + wd]

    q_ref[...] = (seg(0) * (HEAD_DIM ** -0.5)).astype(BF16)
    k = seg(1)
    k_ref[...] = k
    kb_ref[...] = k.astype(BF16)
    v = seg(2)
    v_ref[...] = v
    vb_ref[...] = v.astype(BF16)
    idx_ref[...] = seg(3)
    u_ref[...] = seg(4)
    xr_ref[...] = seg(5)
    xg_ref[...] = seg(6)
    gl_ref[...] = seg(7)


def _in_proj(x, w, b):
    n = x.shape[0]
    tm = min(256, n)
    row = lambda wd: pl.BlockSpec((tm, wd), lambda i: (i, 0))
    full = lambda a: pl.BlockSpec(a.shape, lambda i: (0,) * a.ndim)
    out_shape = (
        jax.ShapeDtypeStruct((n, ATT_W), BF16),
        jax.ShapeDtypeStruct((n, ATT_W), F32),
        jax.ShapeDtypeStruct((n, ATT_W), BF16),
        jax.ShapeDtypeStruct((n, ATT_W), F32),
        jax.ShapeDtypeStruct((n, ATT_W), BF16),
        jax.ShapeDtypeStruct((n, IDX_PAD), F32),
        jax.ShapeDtypeStruct((n, SSM_W), F32),
        jax.ShapeDtypeStruct((n, LRU_W), F32),
        jax.ShapeDtypeStruct((n, LRU_W), F32),
        jax.ShapeDtypeStruct((n, N_BRANCH * D_MODEL), F32),
    )
    return pl.pallas_call(
        _in_proj_kernel,
        grid=(n // tm,),
        in_specs=[row(D_MODEL), full(w), full(b)],
        out_specs=tuple(row(s.shape[1]) for s in out_shape),
        out_shape=out_shape,
        compiler_params=_cparams(("arbitrary",), 56),
    )(x, w, b)


BISECT_STEPS = 17


def _topk_threshold(s_ref, n_chunks, ck, kk, tau_ref, cut_ref):
    rows = s_ref.shape[0]
    nl = ck // LANES

    def chunk(c):
        return s_ref[:, pl.ds(pl.multiple_of(c * ck, LANES), ck)]

    def fold(m, op):
        acc = m[:, 0:LANES]
        for i in range(1, nl):
            acc = op(acc, m[:, i * LANES:(i + 1) * LANES])
        return acc

    def reduce_all(fn, op, init, lane_reduce):
        def body(c, acc):
            return op(acc, fold(fn(chunk(c), c), op))
        acc = lax.fori_loop(0, n_chunks, body, jnp.full((rows, LANES), init, F32))
        return lane_reduce(acc, axis=1, keepdims=True)

    def count(pred):
        return reduce_all(lambda s, c: jnp.where(pred(s, c), 1.0, 0.0), jnp.add, 0.0, jnp.sum)

    row_max = reduce_all(lambda s, c: s, jnp.maximum, -jnp.inf, jnp.max)
    row_min = reduce_all(lambda s, c: jnp.where(s == -jnp.inf, jnp.inf, s), jnp.minimum, jnp.inf, jnp.min)

    lo0 = row_min - 1.0 - jnp.abs(row_min)

    def bisect(_, carry):
        lo, hi = carry
        mid = 0.5 * lo + 0.5 * hi
        ge = count(lambda s, c: s > mid) >= kk
        return jnp.where(ge, mid, lo), jnp.where(ge, hi, mid)

    lo, _ = lax.fori_loop(0, BISECT_STEPS, bisect, (lo0, row_max))

    def snap_cond(carry):
        return carry[2] > 0

    def snap(carry):
        lo, _, _ = carry
        cand = reduce_all(lambda s, c: jnp.where(s > lo, s, jnp.inf), jnp.minimum, jnp.inf, jnp.min)
        done = count(lambda s, c: s > cand) < kk
        pending = jnp.max(jnp.where(done, 0.0, 1.0))
        return jnp.where(done, lo, cand), cand, pending

    _, tau, _ = lax.while_loop(snap_cond, snap, (lo, lo, jnp.float32(1.0)))
    tau_ref[...] = tau

    n_gt = count(lambda s, c: s > tau)
    need = kk - n_gt
    n_eq = count(lambda s, c: s == tau)
    width = s_ref.shape[1]
    cut_ref[...] = jnp.full((rows, 1), width - 1, jnp.int32)
    surplus = jnp.max(jnp.where(n_eq > need, 1.0, 0.0))

    @pl.when(surplus > 0)
    def _():
        lane = lax.broadcasted_iota(jnp.int32, (1, ck), 1)

        def step(_, carry):
            lo_i, hi_i = carry
            mid = (lo_i + hi_i) >> 1
            ge = count(lambda s, c: (s == tau) & (lane + c * ck <= mid)) >= need
            return jnp.where(ge, lo_i, mid), jnp.where(ge, mid, hi_i)

        steps = int(math.ceil(math.log2(width + 1))) + 1
        _, hi_i = lax.fori_loop(0, steps, step, (jnp.full((rows, 1), -1, jnp.int32),
                                                  jnp.full((rows, 1), width - 1, jnp.int32)))
        cut_ref[...] = hi_i


def _selected(s, idx, tau, cut):
    return (s > tau) | ((s == tau) & (idx <= cut))


DSA_TQ = 256
DSA_CK = 512


def _indexer_scores(qi, wi, keys):
    sc = None
    for h in range(IDX_HEADS):
        d = lax.dot_general(qi[:, h * IDX_DIM:(h + 1) * IDX_DIM], keys, (((1,), (1,)), ((), ())),
                            preferred_element_type=F32)
        t = wi[:, h:h + 1] * jnp.maximum(d, 0.0)
        sc = t if sc is None else sc + t
    return sc


def _dsa_index_kernel(q_ref, kw_ref, mask_ref, s_ref, tau_ref, cut_ref, *, topk):
    tq, t_len = s_ref.shape
    ck = min(DSA_CK, t_len)
    i = pl.program_id(1)
    t0 = i * tq
    n_valid = ((i + 1) * tq + ck - 1) // ck
    qi = q_ref[0, :, 0:IDX_KI].astype(BF16)
    wi = q_ref[0, :, IDX_WI:IDX_WI + IDX_HEADS]
    t_ids = t0 + lax.broadcasted_iota(jnp.int32, (tq, 1), 0)
    lane = lax.broadcasted_iota(jnp.int32, (1, ck), 1)

    def score_body(c, carry):
        off = pl.multiple_of(c * ck, LANES)
        keys = kw_ref[0, pl.ds(off, ck), 0:IDX_DIM].astype(BF16)
        sc = _indexer_scores(qi, wi, keys)
        s_ref[:, pl.ds(off, ck)] = jnp.where(lane + c * ck <= t_ids, sc, -jnp.inf)
        return carry

    lax.fori_loop(0, n_valid, score_body, 0)
    kk = jnp.minimum(t_ids + 1, topk).astype(F32)
    _topk_threshold(s_ref, n_valid, ck, kk, tau_ref, cut_ref)
    tau = tau_ref[...]
    cut = cut_ref[...]

    def mask_body(c, carry):
        off = pl.multiple_of(c * ck, LANES)
        sel = _selected(s_ref[:, pl.ds(off, ck)], lane + c * ck, tau, cut)
        mask_ref[0, :, pl.ds(off, ck)] = jnp.where(sel, 1.0, 0.0).astype(BF16)
        return carry

    lax.fori_loop(0, n_valid, mask_body, 0)

    def zero_body(c, carry):
        off = pl.multiple_of(c * ck, LANES)
        mask_ref[0, :, pl.ds(off, ck)] = jnp.zeros((tq, ck), BF16)
        return carry

    lax.fori_loop(n_valid, t_len // ck, zero_body, 0)


def _dsa_index(idx3, topk):
    b, t, _ = idx3.shape
    tq = min(DSA_TQ, t)
    return pl.pallas_call(
        functools.partial(_dsa_index_kernel, topk=topk),
        grid=(b, t // tq),
        in_specs=[pl.BlockSpec((1, tq, IDX_PAD), lambda bi, i: (bi, i, 0)),
                  pl.BlockSpec((1, t, LANES), lambda bi, i: (bi, 0, IDX_KI // LANES))],
        out_specs=pl.BlockSpec((1, tq, t), lambda bi, i: (bi, i, 0)),
        out_shape=jax.ShapeDtypeStruct((b, t, t), BF16),
        scratch_shapes=[pltpu.VMEM((tq, t), F32), pltpu.VMEM((tq, 1), F32), pltpu.VMEM((tq, 1), jnp.int32)],
        compiler_params=_cparams(("arbitrary", "arbitrary"), 48),
    )(idx3, idx3)


def _softmax_block(h, s, msk, v_h, m_ref, l_ref, acc_ref):
    sl = slice(h * HEAD_DIM, (h + 1) * HEAD_DIM)
    s = jnp.where(msk, s, NEG_BIG)
    m_prev = m_ref[h]
    m_new = jnp.maximum(m_prev, jnp.max(s, axis=1, keepdims=True))
    alpha = jnp.exp(m_prev - m_new)
    p = jnp.where(msk, jnp.exp(s - m_new), 0.0)
    l_ref[h] = alpha * l_ref[h] + jnp.sum(p, axis=1, keepdims=True)
    acc_ref[:, sl] = alpha * acc_ref[:, sl] + jnp.dot(p.astype(BF16), v_h, preferred_element_type=F32)
    m_ref[h] = m_new


def _dsa_attn_kernel(q_ref, k_ref, v_ref, mask_ref, o_ref, m_ref, l_ref, acc_ref):
    tq = q_ref.shape[1]
    tk = k_ref.shape[1]
    i = pl.program_id(1)
    j = pl.program_id(2)
    last = ((i + 1) * tq - 1) // tk

    @pl.when(j == 0)
    def _():
        m_ref[...] = jnp.full(m_ref.shape, NEG_BIG, F32)
        l_ref[...] = jnp.zeros(l_ref.shape, F32)
        acc_ref[...] = jnp.zeros(acc_ref.shape, F32)

    @pl.when(j <= last)
    def _():
        msk = mask_ref[0].astype(F32) > 0.5
        q = q_ref[0]
        k = k_ref[0]
        v = v_ref[0]
        for h in range(N_HEADS):
            sl = slice(h * HEAD_DIM, (h + 1) * HEAD_DIM)
            s = lax.dot_general(q[:, sl], k[:, sl], (((1,), (1,)), ((), ())), preferred_element_type=F32)
            _softmax_block(h, s, msk, v[:, sl], m_ref, l_ref, acc_ref)

    @pl.when(j == pl.num_programs(2) - 1)
    def _():
        for h in range(N_HEADS):
            sl = slice(h * HEAD_DIM, (h + 1) * HEAD_DIM)
            o_ref[0, :, sl] = acc_ref[:, sl] / l_ref[h]


def _dsa_attn(q3, k3, v3, mask):
    b, t, _ = q3.shape
    tq = min(DSA_TQ, t)
    tk = min(DSA_CK, t)
    kv_idx = lambda bi, i, j: (bi, jnp.minimum(j, ((i + 1) * tq - 1) // tk), 0)
    return pl.pallas_call(
        _dsa_attn_kernel,
        grid=(b, t // tq, t // tk),
        in_specs=[pl.BlockSpec((1, tq, ATT_W), lambda bi, i, j: (bi, i, 0)),
                  pl.BlockSpec((1, tk, ATT_W), kv_idx),
                  pl.BlockSpec((1, tk, ATT_W), kv_idx),
                  pl.BlockSpec((1, tq, tk), lambda bi, i, j: (bi, i, jnp.minimum(j, ((i + 1) * tq - 1) // tk)))],
        out_specs=pl.BlockSpec((1, tq, ATT_W), lambda bi, i, j: (bi, i, 0)),
        out_shape=jax.ShapeDtypeStruct((b, t, ATT_W), F32),
        scratch_shapes=[pltpu.VMEM((N_HEADS, tq, 1), F32), pltpu.VMEM((N_HEADS, tq, 1), F32),
                        pltpu.VMEM((tq, ATT_W), F32)],
        compiler_params=_cparams(("arbitrary", "arbitrary", "arbitrary"), 32),
    )(q3, k3, v3, mask)


PAGES_PER_STEP = 8
SAMPLE_ROWS = 8


def _sample_index_kernel(pt_ref, q_ref, knew_ref, *refs, topk):
    pages = refs[:PAGES_PER_STEP]
    mask_ref, s_ref, tau_ref, cut_ref = refs[PAGES_PER_STEP:]
    j = pl.program_id(1)
    past = s_ref.shape[1] - PAGE_SIZE
    qi = q_ref[0, :, 0:IDX_KI].astype(BF16)
    wi = q_ref[0, :, IDX_WI:IDX_WI + IDX_HEADS]
    for r in range(PAGES_PER_STEP):
        off = pl.multiple_of((j * PAGES_PER_STEP + r) * PAGE_SIZE, LANES)
        s_ref[:, pl.ds(off, PAGE_SIZE)] = _indexer_scores(qi, wi, pages[r][...].astype(BF16))

    @pl.when(j == pl.num_programs(1) - 1)
    def _():
        sc = _indexer_scores(qi, wi, knew_ref[0].astype(BF16))
        row = lax.broadcasted_iota(jnp.int32, (SAMPLE_ROWS, PAGE_SIZE), 0)
        lane = lax.broadcasted_iota(jnp.int32, (SAMPLE_ROWS, PAGE_SIZE), 1)
        s_ref[:, past:past + PAGE_SIZE] = jnp.where(lane <= row, sc, -jnp.inf)
        width = s_ref.shape[1]
        kk = jnp.full((SAMPLE_ROWS, 1), float(topk), F32)
        _topk_threshold(s_ref, 1, width, kk, tau_ref, cut_ref)
        idx = lax.broadcasted_iota(jnp.int32, (1, width), 1)
        sel = _selected(s_ref[...], idx, tau_ref[...], cut_ref[...])
        mask_ref[0] = jnp.where(sel, 1.0, 0.0).astype(BF16)


def _page_specs(layer, n_pages, block):
    nd = len(block)

    def spec(r):
        def index_map(b, j, pt):
            return (layer, pt[b * n_pages + j * PAGES_PER_STEP + r]) + (0,) * (nd - 2)
        return pl.BlockSpec((None, None) + tuple(block[2:]), index_map)

    return [spec(r) for r in range(PAGES_PER_STEP)]


def _sample_index(pt_flat, idx_pad, knew_pad, cache_kidx, layer, n_pages, topk):
    db = idx_pad.shape[0]
    width = n_pages * PAGE_SIZE + PAGE_SIZE
    grid_spec = pltpu.PrefetchScalarGridSpec(
        num_scalar_prefetch=1,
        grid=(db, n_pages // PAGES_PER_STEP),
        in_specs=[pl.BlockSpec((1, SAMPLE_ROWS, IDX_PAD), lambda b, j, pt: (b, 0, 0)),
                  pl.BlockSpec((1, PAGE_SIZE, IDX_DIM), lambda b, j, pt: (b, 0, 0))]
                 + _page_specs(layer, n_pages, cache_kidx.shape),
        out_specs=pl.BlockSpec((1, SAMPLE_ROWS, width), lambda b, j, pt: (b, 0, 0)),
        scratch_shapes=[pltpu.VMEM((SAMPLE_ROWS, width), F32), pltpu.VMEM((SAMPLE_ROWS, 1), F32),
                        pltpu.VMEM((SAMPLE_ROWS, 1), jnp.int32)],
    )
    return pl.pallas_call(
        functools.partial(_sample_index_kernel, topk=topk),
        grid_spec=grid_spec,
        out_shape=jax.ShapeDtypeStruct((db, SAMPLE_ROWS, width), BF16),
        compiler_params=_cparams(("arbitrary", "arbitrary"), 32),
    )(pt_flat, idx_pad, knew_pad, *([cache_kidx] * PAGES_PER_STEP))


def _sample_attn_kernel(pt_ref, q_ref, mask_ref, mnew_ref, knew_ref, vnew_ref, *refs):
    kp = refs[:PAGES_PER_STEP]
    vp = refs[PAGES_PER_STEP:2 * PAGES_PER_STEP]
    o_ref, kcat_ref, vcat_ref, m_ref, l_ref, acc_ref = refs[2 * PAGES_PER_STEP:]
    j = pl.program_id(1)
    n_q = SAMPLE_ROWS // 2
    rows = n_q * N_HEADS
    head_of_lane = lax.broadcasted_iota(jnp.int32, (N_HEADS, ATT_W), 1) // HEAD_DIM
    head_of_row = lax.broadcasted_iota(jnp.int32, (N_HEADS, ATT_W), 0)
    own = head_of_lane == head_of_row

    @pl.when(j == 0)
    def _():
        m_ref[...] = jnp.full(m_ref.shape, NEG_BIG, F32)
        l_ref[...] = jnp.zeros(l_ref.shape, F32)
        acc_ref[...] = jnp.zeros(acc_ref.shape, F32)

    q = q_ref[0].astype(F32)
    qbd = jnp.concatenate([jnp.where(own, jnp.broadcast_to(q[r:r + 1], (N_HEADS, ATT_W)), 0.0)
                           for r in range(n_q)], axis=0).astype(BF16)

    def update(keys, vals, msk8):
        s = lax.dot_general(qbd, keys, (((1,), (1,)), ((), ())), preferred_element_type=F32)
        mf = msk8.astype(F32)
        msk = jnp.concatenate([jnp.broadcast_to(mf[r:r + 1], (N_HEADS, mf.shape[1])) for r in range(n_q)],
                              axis=0) > 0.5
        s = jnp.where(msk, s, NEG_BIG)
        m_prev = m_ref[...]
        m_new = jnp.maximum(m_prev, jnp.max(s, axis=1, keepdims=True))
        alpha = jnp.exp(m_prev - m_new)
        p = jnp.where(msk, jnp.exp(s - m_new), 0.0)
        l_ref[...] = alpha * l_ref[...] + jnp.sum(p, axis=1, keepdims=True)
        acc_ref[...] = alpha * acc_ref[...] + jnp.dot(p.astype(BF16), vals, preferred_element_type=F32)
        m_ref[...] = m_new

    for r in range(PAGES_PER_STEP):
        kcat_ref[r * PAGE_SIZE:(r + 1) * PAGE_SIZE, :] = kp[r][...].astype(BF16)
        vcat_ref[r * PAGE_SIZE:(r + 1) * PAGE_SIZE, :] = vp[r][...].astype(BF16)
    update(kcat_ref[...], vcat_ref[...], mask_ref[0])

    @pl.when(j == pl.num_programs(1) - 1)
    def _():
        update(knew_ref[0], vnew_ref[0], mnew_ref[0])
        o = acc_ref[...] / l_ref[...]
        outs = [jnp.sum(jnp.where(own, o[r * N_HEADS:(r + 1) * N_HEADS], 0.0), axis=0, keepdims=True)
                for r in range(n_q)]
        o_ref[0] = jnp.concatenate(outs + [jnp.zeros((SAMPLE_ROWS - n_q, ATT_W), F32)], axis=0)


def _sample_attn(pt_flat, q_pad, mask, knew_pad, vnew_pad, cache_k, cache_v, layer, n_pages):
    db = q_pad.shape[0]
    step_w = PAGES_PER_STEP * PAGE_SIZE
    rows = (SAMPLE_ROWS // 2) * N_HEADS
    grid_spec = pltpu.PrefetchScalarGridSpec(
        num_scalar_prefetch=1,
        grid=(db, n_pages // PAGES_PER_STEP),
        in_specs=[pl.BlockSpec((1, SAMPLE_ROWS, ATT_W), lambda b, j, pt: (b, 0, 0)),
                  pl.BlockSpec((1, SAMPLE_ROWS, step_w), lambda b, j, pt: (b, 0, j)),
                  pl.BlockSpec((1, SAMPLE_ROWS, PAGE_SIZE), lambda b, j, pt: (b, 0, n_pages)),
                  pl.BlockSpec((1, PAGE_SIZE, ATT_W), lambda b, j, pt: (b, 0, 0)),
                  pl.BlockSpec((1, PAGE_SIZE, ATT_W), lambda b, j, pt: (b, 0, 0))]
                 + _page_specs(layer, n_pages, cache_k.shape) + _page_specs(layer, n_pages, cache_v.shape),
        out_specs=pl.BlockSpec((1, SAMPLE_ROWS, ATT_W), lambda b, j, pt: (b, 0, 0)),
        scratch_shapes=[pltpu.VMEM((step_w, ATT_W), BF16), pltpu.VMEM((step_w, ATT_W), BF16),
                        pltpu.VMEM((rows, 1), F32), pltpu.VMEM((rows, 1), F32), pltpu.VMEM((rows, ATT_W), F32)],
    )
    return pl.pallas_call(
        _sample_attn_kernel,
        grid_spec=grid_spec,
        out_shape=jax.ShapeDtypeStruct((db, SAMPLE_ROWS, ATT_W), F32),
        compiler_params=_cparams(("arbitrary", "arbitrary"), 32),
    )(pt_flat, q_pad, mask, mask, knew_pad, vnew_pad, *([cache_k] * PAGES_PER_STEP), *([cache_v] * PAGES_PER_STEP))


def _cmul(ar, ai, br, bi):
    return ar * br - ai * bi, ar * bi + ai * br


def _shift_rows(x, s, row, fill):
    return jnp.where(row >= s, pltpu.roll(x, s, 0), fill)


def _s5_kernel(u_ref, h0r_ref, h0i_ref, lam_ref, wb_ref, wc_ref, d_ref, wg_ref, bg_ref,
               z_ref, sr_ref, si_ref, hs_ref, car_ref, *, last_row):
    c = pl.program_id(1)
    chunk = u_ref.shape[1]
    n = SSM_N

    @pl.when(c == 0)
    def _():
        car_ref[:, 0:n] = h0r_ref[0]
        car_ref[:, n:2 * n] = h0i_ref[0]

    lr = lam_ref[0:1, :]
    li = lam_ref[1:2, :]
    dt = jnp.exp(lam_ref[2:3, :])
    mag = jnp.exp(lr * dt)
    p1r = mag * jnp.cos(li * dt)
    p1i = mag * jnp.sin(li * dt)
    den = lr * lr + li * li
    fr = ((p1r - 1.0) * lr + p1i * li) / den
    fi = (p1i * lr - (p1r - 1.0) * li) / den
    p2r, p2i = _cmul(p1r, p1i, p1r, p1i)
    p4r, p4i = _cmul(p2r, p2i, p2r, p2i)
    p8r, p8i = _cmul(p4r, p4i, p4r, p4i)
    row = lax.broadcasted_iota(jnp.int32, (SUBLANES, 1), 0)
    pwr = jnp.ones((SUBLANES, n), F32)
    pwi = jnp.zeros((SUBLANES, n), F32)
    for bit, (qr, qi) in ((1, (p1r, p1i)), (2, (p2r, p2i)), (4, (p4r, p4i)), (8, (p8r, p8i))):
        nr, ni = _cmul(pwr, pwi, qr, qi)
        take = ((row + 1) & bit) != 0
        pwr = jnp.where(take, nr, pwr)
        pwi = jnp.where(take, ni, pwi)

    u = u_ref[0]
    hs_ref[...] = jnp.dot(u.astype(BF16), wb_ref[...], preferred_element_type=F32)

    def group(g, carry):
        cr, ci = carry
        r0 = pl.multiple_of(g * SUBLANES, SUBLANES)
        gr = hs_ref[pl.ds(r0, SUBLANES), 0:n]
        gi = hs_ref[pl.ds(r0, SUBLANES), n:2 * n]
        xr, xi = _cmul(fr, fi, gr, gi)
        for s, (qr, qi) in ((1, (p1r, p1i)), (2, (p2r, p2i)), (4, (p4r, p4i))):
            sr, si = _cmul(qr, qi, _shift_rows(xr, s, row, 0.0), _shift_rows(xi, s, row, 0.0))
            xr, xi = xr + sr, xi + si
        tr, ti = _cmul(pwr, pwi, cr, ci)
        hr, hi = xr + tr, xi + ti
        hs_ref[pl.ds(r0, SUBLANES), 0:n] = hr
        hs_ref[pl.ds(r0, SUBLANES), n:2 * n] = hi
        return hr[SUBLANES - 1:SUBLANES], hi[SUBLANES - 1:SUBLANES]

    cr, ci = lax.fori_loop(0, chunk // SUBLANES, group, (car_ref[:, 0:n], car_ref[:, n:2 * n]))
    car_ref[:, 0:n] = cr
    car_ref[:, n:2 * n] = ci

    y = jnp.dot(hs_ref[...].astype(BF16), wc_ref[...], preferred_element_type=F32) + d_ref[...] * u
    z = jax.nn.gelu(y)
    gate = jax.nn.sigmoid(jnp.dot(z.astype(BF16), wg_ref[...], preferred_element_type=F32) + bg_ref[...])
    z_ref[0] = z * gate

    @pl.when(c == pl.num_programs(1) - 1)
    def _():
        sr_ref[0] = hs_ref[last_row:last_row + 1, 0:n]
        si_ref[0] = hs_ref[last_row:last_row + 1, n:2 * n]


def _blockdiag(blocks):
    g, r, c = blocks.shape
    eye = jnp.eye(g, dtype=blocks.dtype)
    return (blocks[:, :, None, :] * eye[:, None, :, None]).reshape(g * r, g * c)


def _s5(u3, h0r, h0i, t_valid, lam_re, lam_im, log_dt, b_re, b_im, c_re, c_im, d_skip, w_glu, b_glu):
    b, t, _ = u3.shape
    chunk = min(256, t)
    last_row = (t_valid - 1) % chunk
    lam = jnp.concatenate([lam_re.reshape(1, SSM_N), lam_im.reshape(1, SSM_N),
                           jnp.broadcast_to(log_dt[:, None], (SSM_GROUPS, SSM_STATE)).reshape(1, SSM_N),
                           jnp.zeros((SUBLANES - 3, SSM_N), F32)], axis=0)
    wb = jnp.concatenate([_blockdiag(jnp.swapaxes(b_re, 1, 2)), _blockdiag(jnp.swapaxes(b_im, 1, 2))],
                         axis=1).astype(BF16)
    wc = jnp.concatenate([_blockdiag(jnp.swapaxes(c_re, 1, 2)), -_blockdiag(jnp.swapaxes(c_im, 1, 2))],
                         axis=0).astype(BF16)
    full = lambda a: pl.BlockSpec(a.shape, lambda bi, c: (0,) * a.ndim)
    seq = pl.BlockSpec((1, chunk, SSM_W), lambda bi, c: (bi, c, 0))
    st = pl.BlockSpec((1, 1, SSM_N), lambda bi, c: (bi, 0, 0))
    d2 = d_skip.reshape(1, SSM_W)
    wg = w_glu.astype(BF16)
    bg = b_glu.reshape(1, SSM_W)
    return pl.pallas_call(
        functools.partial(_s5_kernel, last_row=last_row),
        grid=(b, t // chunk),
        in_specs=[seq, st, st, full(lam), full(wb), full(wc), full(d2), full(wg), full(bg)],
        out_specs=(seq, st, st),
        out_shape=(jax.ShapeDtypeStruct((b, t, SSM_W), F32), jax.ShapeDtypeStruct((b, 1, SSM_N), F32),
                   jax.ShapeDtypeStruct((b, 1, SSM_N), F32)),
        scratch_shapes=[pltpu.VMEM((chunk, 2 * SSM_N), F32), pltpu.VMEM((1, 2 * SSM_N), F32)],
        compiler_params=_cparams(("arbitrary", "arbitrary"), 48),
    )(u3, h0r.reshape(b, 1, SSM_N), h0i.reshape(b, 1, SSM_N), lam, wb, wc, d2, wg, bg)


def _lru_kernel(x_ref, xg_ref, buf_ref, h0_ref, cw_ref, vec_ref, wa_ref, wx_ref,
                y_ref, nb_ref, hl_ref, ext_ref, a_ref, b_ref, car_ref, *, last_row):
    c = pl.program_id(1)
    chunk = x_ref.shape[1]
    pad = SUBLANES

    @pl.when(c == 0)
    def _():
        ext_ref[0:pad, :] = jnp.zeros((pad, LRU_W), F32)
        ext_ref[pad - (CONV_W - 1):pad, :] = buf_ref[0]
        car_ref[...] = h0_ref[0]

    @pl.when(c > 0)
    def _():
        ext_ref[0:pad, :] = ext_ref[chunk:chunk + pad, :]

    ext_ref[pad:pad + chunk, :] = x_ref[0]
    xc = vec_ref[0:1, :] + cw_ref[CONV_W - 1:CONV_W, :] * x_ref[0]
    for jj in range(CONV_W - 1):
        xc = xc + cw_ref[jj:jj + 1, :] * ext_ref[pad - (CONV_W - 1) + jj:pad - (CONV_W - 1) + jj + chunk, :]
    xb = xc.astype(BF16)
    r = jax.nn.sigmoid(jnp.dot(xb, wa_ref[...], preferred_element_type=F32) + vec_ref[1:2, :])
    gi = jax.nn.sigmoid(jnp.dot(xb, wx_ref[...], preferred_element_type=F32) + vec_ref[2:3, :])
    lam = vec_ref[3:4, :]
    log_sig = -(jnp.maximum(-lam, 0.0) + jnp.log(1.0 + jnp.exp(-jnp.abs(lam))))
    log_a = LRU_C * r * log_sig
    a = jnp.exp(log_a)
    one_minus_a2 = -jnp.tanh(log_a) * (jnp.exp(2.0 * log_a) + 1.0)
    a_ref[...] = a
    b_ref[...] = jnp.sqrt(one_minus_a2) * (gi * xc)
    row = lax.broadcasted_iota(jnp.int32, (SUBLANES, 1), 0)

    def group(g, carry):
        r0 = pl.multiple_of(g * SUBLANES, SUBLANES)
        av = a_ref[pl.ds(r0, SUBLANES), :]
        bv = b_ref[pl.ds(r0, SUBLANES), :]
        for s in (1, 2, 4):
            bv = av * _shift_rows(bv, s, row, 0.0) + bv
            av = av * _shift_rows(av, s, row, 1.0)
        h = bv + av * carry
        b_ref[pl.ds(r0, SUBLANES), :] = h
        return h[SUBLANES - 1:SUBLANES]

    car_ref[...] = lax.fori_loop(0, chunk // SUBLANES, group, car_ref[...])
    y_ref[0] = b_ref[...] * jax.nn.gelu(xg_ref[0])

    @pl.when(c == pl.num_programs(1) - 1)
    def _():
        hl_ref[0] = b_ref[last_row:last_row + 1, :]
        lo = pad + last_row - (CONV_W - 2)
        nb_ref[0] = ext_ref[lo:lo + CONV_W - 1, :]


def _rglru(x3, xg3, conv_buf, h0, t_valid, conv_w, conv_b, w_a, b_a, w_x, b_x, lam):
    b, t, _ = x3.shape
    chunk = min(256, t)
    last_row = (t_valid - 1) % chunk
    vec = jnp.concatenate([conv_b[None], b_a[None], b_x[None], lam[None], jnp.zeros((SUBLANES - 4, LRU_W), F32)], 0)
    wa = _blockdiag(w_a).astype(BF16)
    wx = _blockdiag(w_x).astype(BF16)
    full = lambda a: pl.BlockSpec(a.shape, lambda bi, c: (0,) * a.ndim)
    seq = pl.BlockSpec((1, chunk, LRU_W), lambda bi, c: (bi, c, 0))
    return pl.pallas_call(
        functools.partial(_lru_kernel, last_row=last_row),
        grid=(b, t // chunk),
        in_specs=[seq, seq, pl.BlockSpec((1, CONV_W - 1, LRU_W), lambda bi, c: (bi, 0, 0)),
                  pl.BlockSpec((1, 1, LRU_W), lambda bi, c: (bi, 0, 0)), full(conv_w), full(vec), full(wa), full(wx)],
        out_specs=(seq, pl.BlockSpec((1, CONV_W - 1, LRU_W), lambda bi, c: (bi, 0, 0)),
                   pl.BlockSpec((1, 1, LRU_W), lambda bi, c: (bi, 0, 0))),
        out_shape=(jax.ShapeDtypeStruct((b, t, LRU_W), F32), jax.ShapeDtypeStruct((b, CONV_W - 1, LRU_W), F32),
                   jax.ShapeDtypeStruct((b, 1, LRU_W), F32)),
        scratch_shapes=[pltpu.VMEM((chunk + 2 * SUBLANES, LRU_W), F32), pltpu.VMEM((chunk, LRU_W), F32),
                        pltpu.VMEM((chunk, LRU_W), F32), pltpu.VMEM((1, LRU_W), F32)],
        compiler_params=_cparams(("arbitrary", "arbitrary"), 32),
    )(x3, xg3, conv_buf, h0.reshape(b, 1, LRU_W), conv_w, vec, wa, wx)


def _layer_norm(x, g, b):
    mu = jnp.mean(x, axis=-1, keepdims=True)
    xc = x - mu
    var = jnp.mean(xc * xc, axis=-1, keepdims=True)
    return xc * lax.rsqrt(var + LN_EPS) * g + b


def _mix_kernel(ya_ref, ys_ref, yl_ref, gl_ref, x_ref, wbr_ref, wout_ref, vec_ref, wr_ref, br_ref,
                x1_ref, x1b_ref, comb_ref, pos_ref, cnt_ref, *, alpha, tiles_per_block):
    i = pl.program_id(0)
    tm = x_ref.shape[0]
    mixed = None
    for n, y_ref in enumerate((ya_ref, ys_ref, yl_ref)):
        proj = jnp.dot(y_ref[...].astype(BF16), wbr_ref[n], preferred_element_type=F32)
        t = jax.nn.sigmoid(gl_ref[:, n * D_MODEL:(n + 1) * D_MODEL]) * proj
        mixed = t if mixed is None else mixed + t
    mixed = jnp.dot(mixed.astype(BF16), wout_ref[...], preferred_element_type=F32) + vec_ref[0:1, :]
    x1 = _layer_norm(alpha * x_ref[...] + mixed, vec_ref[1:2, :], vec_ref[2:3, :])
    x1_ref[...] = x1
    x1b_ref[...] = x1.astype(BF16)

    logits = jnp.dot(x1, wr_ref[...], preferred_element_type=F32, precision=lax.Precision.HIGHEST) + br_ref[...]
    lane = lax.broadcasted_iota(jnp.int32, (tm, N_EXPERTS), 1)
    work = logits
    vals, hits = [], []
    for _ in range(TOP_K):
        v = jnp.max(work, axis=1, keepdims=True)
        ix = jnp.min(jnp.where(work == v, lane, N_EXPERTS), axis=1, keepdims=True)
        hit = lane == ix
        vals.append(v)
        hits.append(hit)
        work = jnp.where(hit, -jnp.inf, work)
    es = [jnp.exp(v - vals[0]) for v in vals]
    den = es[0] + es[1] + es[2] + es[3]
    comb = jnp.zeros((tm, N_EXPERTS), F32)
    routed = jnp.zeros((tm, N_EXPERTS), F32)
    for e, hit in zip(es, hits):
        comb = comb + jnp.where(hit, e / den, 0.0)
        routed = routed + jnp.where(hit, 1.0, 0.0)
    comb_ref[...] = comb

    @pl.when(i % tiles_per_block == 0)
    def _():
        cnt_ref[...] = jnp.zeros(cnt_ref.shape, F32)

    before = (lax.broadcasted_iota(jnp.int32, (tm, tm), 1) < lax.broadcasted_iota(jnp.int32, (tm, tm), 0))
    rank = jnp.dot(jnp.where(before, 1.0, 0.0).astype(BF16), routed.astype(BF16), preferred_element_type=F32)
    pos = rank + cnt_ref[...]
    pos_ref[...] = jnp.where(routed > 0.5, pos, -1.0)
    cnt_ref[...] = cnt_ref[...] + jnp.sum(routed, axis=0, keepdims=True)


def _mix(ya, ys, yl, gl, x, w_branch, w_out, b_out, ln_g, ln_b, w_r, b_r, alpha, moe_block):
    n = x.shape[0]
    tm = min(256, n)
    row = lambda wd: pl.BlockSpec((tm, wd), lambda i: (i, 0))
    full = lambda a: pl.BlockSpec(a.shape, lambda i: (0,) * a.ndim)
    vec = jnp.concatenate([b_out[None], ln_g[None], ln_b[None], jnp.zeros((SUBLANES - 3, D_MODEL), F32)], 0)
    wbr = w_branch.astype(BF16)
    wout = w_out.astype(BF16)
    br = b_r.reshape(1, N_EXPERTS)
    return pl.pallas_call(
        functools.partial(_mix_kernel, alpha=alpha, tiles_per_block=moe_block // tm),
        grid=(n // tm,),
        in_specs=[row(ATT_W), row(SSM_W), row(LRU_W), row(N_BRANCH * D_MODEL), row(D_MODEL),
                  full(wbr), full(wout), full(vec), full(w_r), full(br)],
        out_specs=(row(D_MODEL), row(D_MODEL), row(N_EXPERTS), row(N_EXPERTS)),
        out_shape=(jax.ShapeDtypeStruct((n, D_MODEL), F32), jax.ShapeDtypeStruct((n, D_MODEL), BF16),
                   jax.ShapeDtypeStruct((n, N_EXPERTS), F32), jax.ShapeDtypeStruct((n, N_EXPERTS), F32)),
        scratch_shapes=[pltpu.VMEM((1, N_EXPERTS), F32)],
        compiler_params=_cparams(("arbitrary",), 48),
    )(ya, ys, yl, gl, x, wbr, wout, vec, w_r, br)


MOE_BLOCK = 2048
MOE_CHUNK = 256
MOE_SCATTER = 512


def _moe_kernel(cnt_ref, xb_ref, post_ref, pos_ref, comb_ref, wgu_ref, bgu_ref, wdn_ref, bdn_ref, o_ref,
                *, ch, sb):
    tb_i = pl.program_id(0)
    e = pl.program_id(1)
    tb = xb_ref.shape[0]

    @pl.when(e == 0)
    def _():
        o_ref[...] = jnp.zeros(o_ref.shape, F32)

    n_chunks = (cnt_ref[tb_i * N_EXPERTS + e] + ch - 1) // ch
    lane_e = lax.broadcasted_iota(jnp.int32, (1, N_EXPERTS), 1) == e
    slot_row = post_ref[0]

    def chunk_body(c, carry):
        base = c * ch
        pick = (lax.broadcasted_iota(jnp.int32, (ch, tb), 0) + base).astype(F32) == slot_row
        xc = jnp.dot(jnp.where(pick, 1.0, 0.0).astype(BF16), xb_ref[...], preferred_element_type=F32).astype(BF16)
        gu = jnp.dot(xc, wgu_ref[0], preferred_element_type=F32) + bgu_ref[0]
        gate = jnp.minimum(gu[:, :D_FF], SWIGLU_LIMIT)
        up = jnp.clip(gu[:, D_FF:], -SWIGLU_LIMIT, SWIGLU_LIMIT)
        act = (up + 1.0) * gate * jax.nn.sigmoid(SWIGLU_ALPHA * gate)
        y = (jnp.dot(act.astype(BF16), wdn_ref[0], preferred_element_type=F32) + bdn_ref[0]).astype(BF16)
        for s in range(tb // sb):
            rows = slice(s * sb, (s + 1) * sb)
            slot_col = jnp.sum(jnp.where(lane_e, pos_ref[rows, :], 0.0), axis=1, keepdims=True)
            w_col = jnp.sum(jnp.where(lane_e, comb_ref[rows, :], 0.0), axis=1, keepdims=True)
            put = (lax.broadcasted_iota(jnp.int32, (sb, ch), 1) + base).astype(F32) == slot_col
            o_ref[rows, :] += jnp.dot(jnp.where(put, w_col, 0.0).astype(BF16), y, preferred_element_type=F32)
        return carry

    lax.fori_loop(0, n_chunks, chunk_body, 0)


def _moe(x1b, comb, pos, wgu, bgu, wdn, bdn):
    n = x1b.shape[0]
    tb = min(MOE_BLOCK, n)
    ch = min(MOE_CHUNK, tb)
    sb = min(MOE_SCATTER, tb)
    n_tb = n // tb
    routed = (pos >= 0).astype(jnp.int32)
    cnt = routed.reshape(n_tb, tb, N_EXPERTS).sum(axis=1).reshape(-1)
    pos_t = pos.T.reshape(N_EXPERTS, 1, n)
    grid_spec = pltpu.PrefetchScalarGridSpec(
        num_scalar_prefetch=1,
        grid=(n_tb, N_EXPERTS),
        in_specs=[pl.BlockSpec((tb, D_MODEL), lambda t, e, c: (t, 0)),
                  pl.BlockSpec((1, 1, tb), lambda t, e, c: (e, 0, t)),
                  pl.BlockSpec((tb, N_EXPERTS), lambda t, e, c: (t, 0)),
                  pl.BlockSpec((tb, N_EXPERTS), lambda t, e, c: (t, 0)),
                  pl.BlockSpec((1, D_MODEL, 2 * D_FF), lambda t, e, c: (e, 0, 0)),
                  pl.BlockSpec((1, 1, 2 * D_FF), lambda t, e, c: (e, 0, 0)),
                  pl.BlockSpec((1, D_FF, D_MODEL), lambda t, e, c: (e, 0, 0)),
                  pl.BlockSpec((1, 1, D_MODEL), lambda t, e, c: (e, 0, 0))],
        out_specs=pl.BlockSpec((tb, D_MODEL), lambda t, e, c: (t, 0)),
    )
    return pl.pallas_call(
        functools.partial(_moe_kernel, ch=ch, sb=sb),
        grid_spec=grid_spec,
        out_shape=jax.ShapeDtypeStruct((n, D_MODEL), F32),
        compiler_params=_cparams(("arbitrary", "arbitrary"), 56),
    )(cnt, x1b, pos_t, pos, comb, wgu, bgu, wdn, bdn)


def _ln2_kernel(x_ref, m_ref, vec_ref, o_ref, *, alpha):
    o_ref[...] = _layer_norm(alpha * x_ref[...] + m_ref[...], vec_ref[0:1, :], vec_ref[1:2, :])


def _ln2(x1, moe_out, g, b, alpha):
    n = x1.shape[0]
    tm = min(512, n)
    vec = jnp.concatenate([g[None], b[None], jnp.zeros((SUBLANES - 2, D_MODEL), F32)], 0)
    row = pl.BlockSpec((tm, D_MODEL), lambda i: (i, 0))
    return pl.pallas_call(
        functools.partial(_ln2_kernel, alpha=alpha),
        grid=(n // tm,),
        in_specs=[row, row, pl.BlockSpec(vec.shape, lambda i: (0, 0))],
        out_specs=row,
        out_shape=jax.ShapeDtypeStruct((n, D_MODEL), F32),
        compiler_params=_cparams(("arbitrary",), 32),
    )(x1, moe_out, vec)


def _pad_rows(a, rows):
    pad = [(0, 0)] * a.ndim
    pad[1] = (0, rows - a.shape[1])
    return jnp.pad(a, pad)


def kernel(x_prompt, x_sample, cache_k, cache_v, cache_kidx, state_ssm_re, state_ssm_im, state_lru_h, state_lru_conv, page_table, w_in, b_in, ssm_lam_re, ssm_lam_im, ssm_log_dt, ssm_b_re, ssm_b_im, ssm_c_re, ssm_c_im, ssm_d, ssm_w_glu, ssm_b_glu, lru_conv_w, lru_conv_b, lru_w_a, lru_b_a, lru_w_x, lru_b_x, lru_lam, w_branch, w_out, b_out, ln1_g, ln1_b, moe_w_r, moe_b_r, moe_w_gu, moe_b_gu, moe_w_dn, moe_b_dn, ln2_g, ln2_b):
    depth = w_in.shape[0]
    alpha = (2 * depth) ** 0.25
    bp, tp, _ = x_prompt.shape
    db, ns, _ = x_sample.shape
    n_pages = page_table.shape[1]
    past = n_pages * PAGE_SIZE
    n_pool = cache_k.shape[1]
    pt_flat = page_table.reshape(-1).astype(jnp.int32)
    ck2 = cache_k.reshape(depth, n_pool, PAGE_SIZE, ATT_W)
    cv2 = cache_v.reshape(depth, n_pool, PAGE_SIZE, ATT_W)
    topk_p = min(TOPK_MAX, tp // 4)
    topk_s = min(TOPK_MAX, (past + ns) // 4)

    q_end = 3 * ATT_W
    i_end = q_end + IDX_KI + IDX_DIM + IDX_HEADS

    def regroup(a):
        padw = [(0, 0)] * (a.ndim - 1) + [(0, IDX_PAD - (i_end - q_end))]
        return jnp.concatenate([a[..., :q_end], jnp.pad(a[..., q_end:i_end], padw), a[..., i_end:]], axis=-1)

    w_in_r = regroup(w_in).astype(BF16)
    b_in_r = regroup(b_in).reshape(depth, 1, N_IN_PAD)
    wgu_b = moe_w_gu.astype(BF16)
    wdn_b = moe_w_dn.astype(BF16)
    bgu3 = moe_b_gu.reshape(depth, N_EXPERTS, 1, 2 * D_FF)
    bdn3 = moe_b_dn.reshape(depth, N_EXPERTS, 1, D_MODEL)

    xp = x_prompt.reshape(bp * tp, D_MODEL)
    xs = x_sample.reshape(db * ns, D_MODEL)
    new_p = [[] for _ in range(7)]
    new_s = [[] for _ in range(7)]

    def channel_mix(l, x, ya, ys, yl, gl):
        n = x.shape[0]
        x1, x1b, comb, pos = _mix(ya, ys, yl, gl, x, w_branch[l], w_out[l], b_out[l], ln1_g[l], ln1_b[l],
                                  moe_w_r[l], moe_b_r[l], alpha, min(MOE_BLOCK, n))
        moe_out = _moe(x1b, comb, pos, wgu_b[l], bgu3[l], wdn_b[l], bdn3[l])
        return _ln2(x1, moe_out, ln2_g[l], ln2_b[l], alpha)

    for l in range(depth):
        ssm_p = (ssm_lam_re[l], ssm_lam_im[l], ssm_log_dt[l], ssm_b_re[l], ssm_b_im[l], ssm_c_re[l],
                 ssm_c_im[l], ssm_d[l], ssm_w_glu[l], ssm_b_glu[l])
        lru_p = (lru_conv_w[l], lru_conv_b[l], lru_w_a[l], lru_b_a[l], lru_w_x[l], lru_b_x[l], lru_lam[l])

        q, k, kb, v, vb, idx, u, xr, xg, gl = _in_proj(xp, w_in_r[l], b_in_r[l])
        r3 = lambda a: a.reshape(bp, tp, a.shape[-1])
        idx3 = r3(idx)
        mask = _dsa_index(idx3, topk_p)
        ya = _dsa_attn(r3(q), r3(kb), r3(vb), mask).reshape(bp * tp, ATT_W)
        zeros_n = jnp.zeros((bp, SSM_N), F32)
        ys, s_re, s_im = _s5(r3(u), zeros_n, zeros_n, tp, *ssm_p)
        yl, nbuf, hl = _rglru(r3(xr), r3(xg), jnp.zeros((bp, CONV_W - 1, LRU_W), F32), jnp.zeros((bp, LRU_W), F32),
                              tp, *lru_p)
        st = (k.reshape(bp, tp, N_HEADS, HEAD_DIM), v.reshape(bp, tp, N_HEADS, HEAD_DIM),
              idx3[:, :, IDX_KI:IDX_KI + IDX_DIM], s_re.reshape(bp, SSM_GROUPS, SSM_STATE),
              s_im.reshape(bp, SSM_GROUPS, SSM_STATE), nbuf, hl.reshape(bp, LRU_W))
        for jj in range(7):
            new_p[jj].append(st[jj])
        xp = channel_mix(l, xp, ya, ys.reshape(bp * tp, SSM_W), yl.reshape(bp * tp, LRU_W), gl)

        q, k, kb, v, vb, idx, u, xr, xg, gl = _in_proj(xs, w_in_r[l], b_in_r[l])
        r3 = lambda a: a.reshape(db, ns, a.shape[-1])
        idx3 = r3(idx)
        ki_new = idx3[:, :, IDX_KI:IDX_KI + IDX_DIM]
        smask = _sample_index(pt_flat, _pad_rows(idx3, SAMPLE_ROWS), _pad_rows(ki_new, PAGE_SIZE), cache_kidx, l,
                              n_pages, topk_s)
        ya = _sample_attn(pt_flat, _pad_rows(r3(q), SAMPLE_ROWS), smask, _pad_rows(r3(kb), PAGE_SIZE),
                          _pad_rows(r3(vb), PAGE_SIZE), ck2, cv2, l, n_pages)[:, :ns].reshape(db * ns, ATT_W)
        ys, s_re, s_im = _s5(_pad_rows(r3(u), SAMPLE_ROWS), state_ssm_re[l].reshape(db, SSM_N),
                             state_ssm_im[l].reshape(db, SSM_N), ns, *ssm_p)
        yl, nbuf, hl = _rglru(_pad_rows(r3(xr), SAMPLE_ROWS), _pad_rows(r3(xg), SAMPLE_ROWS), state_lru_conv[l],
                              state_lru_h[l], ns, *lru_p)
        st = (k.reshape(db, ns, N_HEADS, HEAD_DIM), v.reshape(db, ns, N_HEADS, HEAD_DIM), ki_new,
              s_re.reshape(db, SSM_GROUPS, SSM_STATE), s_im.reshape(db, SSM_GROUPS, SSM_STATE), nbuf,
              hl.reshape(db, LRU_W))
        for jj in range(7):
            new_s[jj].append(st[jj])
        xs = channel_mix(l, xs, ya, ys[:, :ns].reshape(db * ns, SSM_W), yl[:, :ns].reshape(db * ns, LRU_W), gl)

    k_p, v_p, ki_p, sre_p, sim_p, lc_p, lh_p = [jnp.stack(a) for a in new_p]
    k_s, v_s, ki_s, sre_s, sim_s, lc_s, lh_s = [jnp.stack(a) for a in new_s]
    return (xp.reshape(bp, tp, D_MODEL), xs.reshape(db, ns, D_MODEL), k_p, v_p, ki_p, sre_p, sim_p, lh_p, lc_p,
            k_s, v_s, ki_s, sre_s, sim_s, lh_s, lc_s)
```

```python
import functools
import math

import jax
import jax.numpy as jnp
from jax import lax
from jax.experimental import pallas as pl
from jax.experimental.pallas import tpu as pltpu

F32 = jnp.float32
BF16 = jnp.bfloat16

D_MODEL = 1024
N_HEADS = 8
HEAD_DIM = 64
ATT_W = N_HEADS * HEAD_DIM
IDX_HEADS = 4
IDX_DIM = 64
TOPK_MAX = 256
PAGE_SIZE = 128
SSM_W = 512
SSM_GROUPS = 32
SSM_GROUP = 16
SSM_STATE = 64
SSM_N = SSM_GROUPS * SSM_STATE
LRU_W = 512
CONV_W = 4
LRU_C = 8.0
N_EXPERTS = 32
TOP_K = 4
D_FF = 1024
SWIGLU_LIMIT = 7.0
SWIGLU_ALPHA = 1.702
N_BRANCH = 3
LN_EPS = 1e-5
IDX_PAD = 384
IDX_KI = IDX_HEADS * IDX_DIM
IDX_WI = IDX_KI + IDX_DIM
NEG_BIG = -1e30
M_INIT = -1e29
QK_SCALE = HEAD_DIM ** -0.5 * math.log2(math.e)
SUBLANES = 8
LANES = 128


def _cparams(sem, vmem_mb):
    return pltpu.CompilerParams(dimension_semantics=sem, vmem_limit_bytes=vmem_mb * 1024 * 1024)


_SEG_W = (ATT_W, ATT_W, ATT_W, IDX_PAD, SSM_W, LRU_W, LRU_W, N_BRANCH * D_MODEL)
_SEG_OFF = tuple(sum(_SEG_W[:i]) for i in range(len(_SEG_W)))
N_IN_PAD = sum(_SEG_W)


def _in_proj_kernel(x_ref, w_ref, b_ref, q_ref, k_ref, kb_ref, v_ref, vb_ref, idx_ref, u_ref, xr_ref,
                    xg_ref, gl_ref):
    xb = x_ref[...].astype(BF16)

    def seg(i):
        off, wd = _SEG_OFF[i], _SEG_W[i]
        return jnp.dot(xb, w_ref[:, off:off + wd], preferred_element_type=F32) + b_ref[:, off:off + wd]

    tm = x_ref.shape[0]
    q = seg(0) * QK_SCALE
    k = seg(1)
    v = seg(2)
    k_ref[...] = k
    v_ref[...] = v
    ones = jnp.ones((tm, HEAD_DIM), F32)
    for h in range(N_HEADS):
        sl = slice(h * HEAD_DIM, (h + 1) * HEAD_DIM)
        q_ref[h] = q[:, sl].astype(BF16)
        kb_ref[h] = k[:, sl].astype(BF16)
        vb_ref[h] = jnp.concatenate([v[:, sl], ones], axis=1).astype(BF16)
    idx_ref[...] = seg(3)
    u_ref[...] = seg(4)
    xr_ref[...] = seg(5)
    xg_ref[...] = seg(6)
    gl_ref[...] = seg(7)


def _in_proj(x, w, b):
    n = x.shape[0]
    tm = min(256, n)
    row = lambda wd: pl.BlockSpec((tm, wd), lambda i: (i, 0))
    full = lambda a: pl.BlockSpec(a.shape, lambda i: (0,) * a.ndim)
    heads = lambda wd: pl.BlockSpec((N_HEADS, tm, wd), lambda i: (0, i, 0))
    out_shape = (
        jax.ShapeDtypeStruct((N_HEADS, n, HEAD_DIM), BF16),
        jax.ShapeDtypeStruct((n, ATT_W), F32),
        jax.ShapeDtypeStruct((N_HEADS, n, HEAD_DIM), BF16),
        jax.ShapeDtypeStruct((n, ATT_W), F32),
        jax.ShapeDtypeStruct((N_HEADS, n, 2 * HEAD_DIM), BF16),
        jax.ShapeDtypeStruct((n, IDX_PAD), F32),
        jax.ShapeDtypeStruct((n, SSM_W), F32),
        jax.ShapeDtypeStruct((n, LRU_W), F32),
        jax.ShapeDtypeStruct((n, LRU_W), F32),
        jax.ShapeDtypeStruct((n, N_BRANCH * D_MODEL), F32),
    )
    return pl.pallas_call(
        _in_proj_kernel,
        grid=(n // tm,),
        in_specs=[row(D_MODEL), full(w), full(b)],
        out_specs=tuple(heads(s.shape[2]) if len(s.shape) == 3 else row(s.shape[1]) for s in out_shape),
        out_shape=out_shape,
        compiler_params=_cparams(("arbitrary",), 56),
        name="in_proj",
    )(x, w, b)


BISECT_STEPS = 20


def _prefix_matrix(ck):
    r = lax.broadcasted_iota(jnp.int32, (ck, ck), 0)
    c = lax.broadcasted_iota(jnp.int32, (ck, ck), 1)
    return jnp.where(r <= c, 1.0, 0.0).astype(BF16)


def _select_rows(s_ref, rows, rb, n_chunks, ck, kk, n_adm):
    nl = ck // LANES

    def tile(c, j):
        return s_ref[rows, pl.ds(pl.multiple_of(c * ck + j * LANES, LANES), LANES)]

    def lanes(x):
        return jnp.broadcast_to(x, (rb, LANES))

    def count_gt(x):
        xb = lanes(x)

        def body(c, acc):
            for j in range(nl):
                acc = acc + lax.shift_right_arithmetic(pltpu.bitcast(xb - tile(c, j), jnp.int32), 31)
            return acc
        neg = lax.fori_loop(0, n_chunks, body, jnp.zeros((rb, LANES), jnp.int32))
        return -jnp.sum(neg.astype(F32), axis=1, keepdims=True)

    def min_above_max_upto(lo, hi):
        lob, hib = lanes(lo), lanes(hi)

        def body(c, acc):
            mn, mx = acc
            for j in range(nl):
                v = tile(c, j)
                mn = jnp.minimum(mn, jnp.where(v > lob, v, jnp.inf))
                mx = jnp.maximum(mx, jnp.where(v <= hib, v, -jnp.inf))
            return mn, mx
        mn, mx = lax.fori_loop(0, n_chunks, body, (jnp.full((rb, LANES), jnp.inf, F32),
                                                   jnp.full((rb, LANES), -jnp.inf, F32)))
        return jnp.min(mn, axis=1, keepdims=True), jnp.max(mx, axis=1, keepdims=True)

    def bisect_at(x, st):
        lo, hi, clo, chi = st
        c = count_gt(x)
        ge = c >= kk
        return (jnp.where(ge, x, lo), jnp.where(ge, hi, x), jnp.where(ge, c, clo), jnp.where(ge, chi, c))

    inf = jnp.full((rb, 1), jnp.inf, F32)
    row_min, row_max = min_above_max_upto(-inf, inf)
    st = (row_min - 1.0 - jnp.abs(row_min), row_max, n_adm, jnp.zeros((rb, 1), F32))
    st = lax.fori_loop(0, BISECT_STEPS, lambda _, s: bisect_at(0.5 * s[0] + 0.5 * s[1], s), st)

    def refine_cond(carry):
        return carry[3] > 0.5

    def refine(carry):
        st, a, b, _ = carry
        x = 0.5 * a + 0.5 * b
        st = bisect_at(jnp.where(x >= b, a, x), st)
        a, b = min_above_max_upto(st[0], st[1])
        return st, a, b, jnp.max(jnp.where(a == b, 0.0, 1.0))

    a, b = min_above_max_upto(st[0], st[1])
    st, tau, _, _ = lax.while_loop(refine_cond, refine, (st, a, b, jnp.max(jnp.where(a == b, 0.0, 1.0))))

    return tau, kk - st[3]


def _emit_selection(s_ref, n_chunks, ck, tau, need, emit):
    rows = s_ref.shape[0]
    tri = _prefix_matrix(ck)

    def body(c, seen):
        v = s_ref[:, pl.ds(pl.multiple_of(c * ck, LANES), ck)]
        eq = v == tau
        pref = jnp.dot(jnp.where(eq, 1.0, 0.0).astype(BF16), tri, preferred_element_type=F32) + seen
        emit(c, (v > tau) | (eq & (pref <= need)))
        return pref[:, ck - 1:ck]

    lax.fori_loop(0, n_chunks, body, jnp.zeros((rows, 1), F32))


def _indexer_scores(qi, wi, keys):
    sc = None
    for h in range(IDX_HEADS):
        d = lax.dot_general(qi[:, h * IDX_DIM:(h + 1) * IDX_DIM], keys, (((1,), (1,)), ((), ())),
                            preferred_element_type=F32)
        t = wi[:, h:h + 1] * jnp.maximum(d, 0.0)
        sc = t if sc is None else sc + t
    return sc


DSA_TQ = 256
DSA_CK = 512
DSA_RB = 64


def _dsa_index_kernel(q_ref, kw_ref, bias_ref, s_ref, tau_ref, need_ref, *, topk):
    tq, t_len = s_ref.shape
    ck = min(DSA_CK, t_len)
    rb = min(DSA_RB, tq)
    i = pl.program_id(1)
    t0 = i * tq
    n_valid = ((i + 1) * tq + ck - 1) // ck
    qi = q_ref[0, :, 0:IDX_KI].astype(BF16)
    wi = q_ref[0, :, IDX_WI:IDX_WI + IDX_HEADS]
    t_ids = t0 + lax.broadcasted_iota(jnp.int32, (tq, 1), 0)
    lane = lax.broadcasted_iota(jnp.int32, (1, ck), 1)

    def score_body(c, carry):
        off = pl.multiple_of(c * ck, LANES)
        keys = kw_ref[0, pl.ds(off, ck), 0:IDX_DIM].astype(BF16)
        sc = _indexer_scores(qi, wi, keys)
        s_ref[:, pl.ds(off, ck)] = jnp.where(lane + c * ck <= t_ids, sc, -jnp.inf)
        return carry

    lax.fori_loop(0, n_valid, score_body, 0)

    pair = (t_len // ck) % 2 == 0
    if pair:
        @pl.when(n_valid % 2 == 1)
        def _():
            s_ref[:, pl.ds(pl.multiple_of(n_valid * ck, LANES), ck)] = jnp.full((tq, ck), -jnp.inf, F32)
    n_search, ck_search = ((n_valid + 1) // 2, 2 * ck) if pair else (n_valid, ck)

    def block_body(bi, carry):
        r0 = pl.multiple_of(bi * rb, rb)
        rows = pl.ds(r0, rb)
        n_adm = (t0 + r0 + 1 + lax.broadcasted_iota(jnp.int32, (rb, 1), 0)).astype(F32)
        tau, need = _select_rows(s_ref, rows, rb, n_search, ck_search, jnp.minimum(n_adm, float(topk)), n_adm)
        tau_ref[rows, :] = tau
        need_ref[rows, :] = need
        return carry

    lax.fori_loop(0, tq // rb, block_body, 0)

    def emit(c, keep):
        bias_ref[0, :, pl.ds(pl.multiple_of(c * ck, LANES), ck)] = jnp.where(keep, 0.0, NEG_BIG).astype(BF16)

    _emit_selection(s_ref, n_valid, ck, tau_ref[...], need_ref[...], emit)

    def fill_body(c, carry):
        bias_ref[0, :, pl.ds(pl.multiple_of(c * ck, LANES), ck)] = jnp.full((tq, ck), NEG_BIG, BF16)
        return carry

    lax.fori_loop(n_valid, t_len // ck, fill_body, 0)


def _dsa_index(idx3, topk):
    b, t, _ = idx3.shape
    tq = min(DSA_TQ, t)
    return pl.pallas_call(
        functools.partial(_dsa_index_kernel, topk=topk),
        grid=(b, t // tq),
        in_specs=[pl.BlockSpec((1, tq, IDX_PAD), lambda bi, i: (bi, i, 0)),
                  pl.BlockSpec((1, t, LANES), lambda bi, i: (bi, 0, IDX_KI // LANES))],
        out_specs=pl.BlockSpec((1, tq, t), lambda bi, i: (bi, i, 0)),
        out_shape=jax.ShapeDtypeStruct((b, t, t), BF16),
        scratch_shapes=[pltpu.VMEM((tq, t), F32), pltpu.VMEM((tq, 1), F32), pltpu.VMEM((tq, 1), F32)],
        compiler_params=_cparams(("arbitrary", "arbitrary"), 48),
        name="dsa_index",
    )(idx3, idx3)


def _dsa_attn_kernel(q_ref, k_ref, v_ref, bias_ref, o_ref, m_ref, acc_ref, bias_scr):
    tq = q_ref.shape[1]
    tk = k_ref.shape[1]
    i = pl.program_id(1)
    j = pl.program_id(2)
    last = ((i + 1) * tq - 1) // tk

    @pl.when(j == 0)
    def _():
        m_ref[...] = jnp.full(m_ref.shape, M_INIT, F32)
        acc_ref[...] = jnp.zeros(acc_ref.shape, F32)

    @pl.when(j <= last)
    def _():
        bias_scr[...] = bias_ref[0].astype(F32)
        for h in range(N_HEADS):
            s = lax.dot_general(q_ref[h], k_ref[h], (((1,), (1,)), ((), ())), preferred_element_type=F32)
            s = s + bias_scr[...]
            m_prev = m_ref[h]
            m_new = jnp.maximum(m_prev, jnp.max(s, axis=1, keepdims=True))
            p = jnp.exp2(s - m_new)
            acc_ref[h] = jnp.exp2(m_prev - m_new) * acc_ref[h] + jnp.dot(p.astype(BF16), v_ref[h],
                                                                         preferred_element_type=F32)
            m_ref[h] = m_new

    @pl.when(j == pl.num_programs(2) - 1)
    def _():
        for h in range(N_HEADS):
            acc = acc_ref[h]
            o_ref[0, :, h * HEAD_DIM:(h + 1) * HEAD_DIM] = acc[:, 0:HEAD_DIM] / acc[:, HEAD_DIM:HEAD_DIM + 1]


def _dsa_attn(q_hm, k_hm, v_hm, bias):
    b, t, _ = bias.shape
    tq = min(DSA_TQ, t)
    tk = min(DSA_CK, t)
    nq, nk = t // tq, t // tk
    last = lambda i: ((i + 1) * tq - 1) // tk
    kv_idx = lambda bi, i, j: (0, bi * nk + jnp.minimum(j, last(i)), 0)
    return pl.pallas_call(
        _dsa_attn_kernel,
        grid=(b, nq, nk),
        in_specs=[pl.BlockSpec((N_HEADS, tq, HEAD_DIM), lambda bi, i, j: (0, bi * nq + i, 0)),
                  pl.BlockSpec((N_HEADS, tk, HEAD_DIM), kv_idx),
                  pl.BlockSpec((N_HEADS, tk, 2 * HEAD_DIM), kv_idx),
                  pl.BlockSpec((1, tq, tk), lambda bi, i, j: (bi, i, jnp.minimum(j, last(i))))],
        out_specs=pl.BlockSpec((1, tq, ATT_W), lambda bi, i, j: (bi, i, 0)),
        out_shape=jax.ShapeDtypeStruct((b, t, ATT_W), F32),
        scratch_shapes=[pltpu.VMEM((N_HEADS, tq, 1), F32), pltpu.VMEM((N_HEADS, tq, 2 * HEAD_DIM), F32),
                        pltpu.VMEM((tq, tk), F32)],
        compiler_params=_cparams(("arbitrary", "arbitrary", "arbitrary"), 32),
        name="dsa_attn",
    )(q_hm, k_hm, v_hm, bias)


PAGES_PER_STEP = 8
SAMPLE_ROWS = 8
SAMPLE_CK = 640


def _sample_index_kernel(pt_ref, q_ref, knew_ref, *refs, topk, n_new):
    pages = refs[:PAGES_PER_STEP]
    keep_ref, s_ref = refs[PAGES_PER_STEP:]
    j = pl.program_id(1)
    width = s_ref.shape[1]
    past = width - PAGE_SIZE
    step_w = PAGES_PER_STEP * PAGE_SIZE
    q = q_ref[0]
    wi = q[:, IDX_WI:IDX_WI + IDX_HEADS]
    q4 = jnp.concatenate([q[:, h * IDX_DIM:(h + 1) * IDX_DIM] for h in range(IDX_HEADS)], axis=0).astype(BF16)

    def scores(keys):
        d = lax.dot_general(q4, keys, (((1,), (1,)), ((), ())), preferred_element_type=F32)
        sc = None
        for h in range(IDX_HEADS):
            t = wi[:, h:h + 1] * jnp.maximum(d[h * SAMPLE_ROWS:(h + 1) * SAMPLE_ROWS], 0.0)
            sc = t if sc is None else sc + t
        return sc

    keys = jnp.concatenate([p[...] for p in pages], axis=0).astype(BF16)
    s_ref[:, pl.ds(pl.multiple_of(j * step_w, LANES), step_w)] = scores(keys)

    @pl.when(j == pl.num_programs(1) - 1)
    def _():
        sc = scores(knew_ref[0].astype(BF16))
        row = lax.broadcasted_iota(jnp.int32, (SAMPLE_ROWS, PAGE_SIZE), 0)
        lane = lax.broadcasted_iota(jnp.int32, (SAMPLE_ROWS, PAGE_SIZE), 1)
        ok = (lane <= row) & (lane < n_new)
        s_ref[:, past:past + PAGE_SIZE] = jnp.where(ok, sc, -jnp.inf)
        r1 = lax.broadcasted_iota(jnp.int32, (SAMPLE_ROWS, 1), 0)
        n_adm = (past + jnp.minimum(r1 + 1, n_new)).astype(F32)

        def emit(c, keep):
            keep_ref[0, :, pl.ds(pl.multiple_of(c * SAMPLE_CK, LANES), SAMPLE_CK)] = jnp.where(keep, 1.0, 0.0).astype(BF16)

        tau, need = _select_rows(s_ref, pl.ds(0, SAMPLE_ROWS), SAMPLE_ROWS, width // SAMPLE_CK, SAMPLE_CK,
                                 jnp.full((SAMPLE_ROWS, 1), float(topk), F32), n_adm)
        _emit_selection(s_ref, width // SAMPLE_CK, SAMPLE_CK, tau, need, emit)


def _page_specs(layer, n_pages, block):
    def spec(r):
        def index_map(b, j, pt):
            return (layer, pt[b * n_pages + j * PAGES_PER_STEP + r], 0, 0)
        return pl.BlockSpec((None, None) + tuple(block), index_map)

    return [spec(r) for r in range(PAGES_PER_STEP)]


def _sample_index(pt_flat, idx_pad, knew_pad, cache_kidx, layer, n_pages, topk, n_new):
    db = idx_pad.shape[0]
    width = n_pages * PAGE_SIZE + PAGE_SIZE
    assert width % SAMPLE_CK == 0 and n_pages % PAGES_PER_STEP == 0
    grid_spec = pltpu.PrefetchScalarGridSpec(
        num_scalar_prefetch=1,
        grid=(db, n_pages // PAGES_PER_STEP),
        in_specs=[pl.BlockSpec((1, SAMPLE_ROWS, IDX_PAD), lambda b, j, pt: (b, 0, 0)),
                  pl.BlockSpec((1, PAGE_SIZE, IDX_DIM), lambda b, j, pt: (b, 0, 0))]
                 + _page_specs(layer, n_pages, (PAGE_SIZE, IDX_DIM)),
        out_specs=pl.BlockSpec((1, SAMPLE_ROWS, width), lambda b, j, pt: (b, 0, 0)),
        scratch_shapes=[pltpu.VMEM((SAMPLE_ROWS, width), F32)],
    )
    return pl.pallas_call(
        functools.partial(_sample_index_kernel, topk=topk, n_new=n_new),
        grid_spec=grid_spec,
        out_shape=jax.ShapeDtypeStruct((db, SAMPLE_ROWS, width), BF16),
        compiler_params=_cparams(("arbitrary", "arbitrary"), 32),
        name="sample_index",
    )(pt_flat, idx_pad, knew_pad, *([cache_kidx] * PAGES_PER_STEP))


SAMPLE_Q = 4
SAMPLE_LINES = SAMPLE_Q * N_HEADS
KEY_LINES = PAGE_SIZE * N_HEADS


def _sample_attn_kernel(pt_ref, q_ref, keep_ref, keepn_ref, knew_ref, vnew_ref, *refs):
    kp = refs[:PAGES_PER_STEP]
    vp = refs[PAGES_PER_STEP:2 * PAGES_PER_STEP]
    o_ref, m_ref, l_ref, acc_ref = refs[2 * PAGES_PER_STEP:]
    j = pl.program_id(1)

    @pl.when(j == 0)
    def _():
        m_ref[...] = jnp.full(m_ref.shape, M_INIT, F32)
        l_ref[...] = jnp.zeros(l_ref.shape, F32)
        acc_ref[...] = jnp.zeros(acc_ref.shape, F32)

    q = q_ref[0]
    same_head = (lax.broadcasted_iota(jnp.int32, (SAMPLE_LINES, KEY_LINES), 0) % N_HEADS
                 == lax.broadcasted_iota(jnp.int32, (SAMPLE_LINES, KEY_LINES), 1) % N_HEADS)
    spread = jnp.where(lax.broadcasted_iota(jnp.int32, (PAGE_SIZE, KEY_LINES), 1) // N_HEADS
                       == lax.broadcasted_iota(jnp.int32, (PAGE_SIZE, KEY_LINES), 0), 1.0, 0.0).astype(BF16)

    def page_logits(k_page, keep8):
        s = lax.dot_general(q, k_page, (((1,), (1,)), ((), ())), preferred_element_type=F32)
        k8 = jnp.dot(keep8, spread, preferred_element_type=F32)
        kl = jnp.concatenate([jnp.broadcast_to(k8[r:r + 1], (N_HEADS, KEY_LINES)) for r in range(SAMPLE_Q)], axis=0)
        return jnp.where(same_head & (kl > 0.5), s, NEG_BIG)

    def update(logits, v_pages):
        m_prev = m_ref[...]
        m_new = m_prev
        for s in logits:
            m_new = jnp.maximum(m_new, jnp.max(s, axis=1, keepdims=True))
        alpha = jnp.exp2(m_prev - m_new)
        l_new = alpha * l_ref[...]
        acc = alpha * acc_ref[...]
        for s, v_page in zip(logits, v_pages):
            p = jnp.exp2(s - m_new)
            l_new = l_new + jnp.sum(p, axis=1, keepdims=True)
            acc = acc + jnp.dot(p.astype(BF16), v_page, preferred_element_type=F32)
        m_ref[...] = m_new
        l_ref[...] = l_new
        acc_ref[...] = acc

    keep = keep_ref[0]
    update([page_logits(kp[r][...].astype(BF16), keep[:, r * PAGE_SIZE:(r + 1) * PAGE_SIZE])
            for r in range(PAGES_PER_STEP)], [vp[r][...].astype(BF16) for r in range(PAGES_PER_STEP)])

    @pl.when(j == pl.num_programs(1) - 1)
    def _():
        update([page_logits(knew_ref[0], keepn_ref[0])], [vnew_ref[0]])
        o_ref[0] = acc_ref[...] / l_ref[...]


def _sample_attn(pt_flat, q_lines, keep, knew_lines, vnew_lines, cache_k, cache_v, layer, n_pages):
    db = q_lines.shape[0]
    step_w = PAGES_PER_STEP * PAGE_SIZE
    grid_spec = pltpu.PrefetchScalarGridSpec(
        num_scalar_prefetch=1,
        grid=(db, n_pages // PAGES_PER_STEP),
        in_specs=[pl.BlockSpec((1, SAMPLE_LINES, HEAD_DIM), lambda b, j, pt: (b, 0, 0)),
                  pl.BlockSpec((1, SAMPLE_ROWS, step_w), lambda b, j, pt: (b, 0, j)),
                  pl.BlockSpec((1, SAMPLE_ROWS, PAGE_SIZE), lambda b, j, pt: (b, 0, n_pages)),
                  pl.BlockSpec((1, KEY_LINES, HEAD_DIM), lambda b, j, pt: (b, 0, 0)),
                  pl.BlockSpec((1, KEY_LINES, HEAD_DIM), lambda b, j, pt: (b, 0, 0))]
                 + _page_specs(layer, n_pages, (KEY_LINES, HEAD_DIM)) + _page_specs(layer, n_pages, (KEY_LINES, HEAD_DIM)),
        out_specs=pl.BlockSpec((1, SAMPLE_LINES, HEAD_DIM), lambda b, j, pt: (b, 0, 0)),
        scratch_shapes=[pltpu.VMEM((SAMPLE_LINES, 1), F32), pltpu.VMEM((SAMPLE_LINES, 1), F32),
                        pltpu.VMEM((SAMPLE_LINES, HEAD_DIM), F32)],
    )
    return pl.pallas_call(
        _sample_attn_kernel,
        grid_spec=grid_spec,
        out_shape=jax.ShapeDtypeStruct((db, SAMPLE_LINES, HEAD_DIM), F32),
        compiler_params=_cparams(("arbitrary", "arbitrary"), 48),
        name="sample_attn",
    )(pt_flat, q_lines, keep, keep, knew_lines, vnew_lines, *([cache_k] * PAGES_PER_STEP),
      *([cache_v] * PAGES_PER_STEP))


def _cmul(ar, ai, br, bi):
    return ar * br - ai * bi, ar * bi + ai * br


def _shift_rows(x, s, row, fill):
    return jnp.where(row >= s, pltpu.roll(x, s, 0), fill)


def _s5_kernel(u_ref, h0r_ref, h0i_ref, lam_ref, wb_ref, wc_ref, d_ref, wg_ref, bg_ref,
               z_ref, sr_ref, si_ref, hs_ref, car_ref, *, last_row):
    c = pl.program_id(1)
    chunk = u_ref.shape[1]
    n = SSM_N

    @pl.when(c == 0)
    def _():
        car_ref[:, 0:n] = h0r_ref[0]
        car_ref[:, n:2 * n] = h0i_ref[0]

    lr = lam_ref[0:1, :]
    li = lam_ref[1:2, :]
    dt = jnp.exp(lam_ref[2:3, :])
    mag = jnp.exp(lr * dt)
    p1r = mag * jnp.cos(li * dt)
    p1i = mag * jnp.sin(li * dt)
    den = lr * lr + li * li
    fr = ((p1r - 1.0) * lr + p1i * li) / den
    fi = (p1i * lr - (p1r - 1.0) * li) / den
    p2r, p2i = _cmul(p1r, p1i, p1r, p1i)
    p4r, p4i = _cmul(p2r, p2i, p2r, p2i)
    p8r, p8i = _cmul(p4r, p4i, p4r, p4i)
    row = lax.broadcasted_iota(jnp.int32, (SUBLANES, 1), 0)
    pwr = jnp.ones((SUBLANES, n), F32)
    pwi = jnp.zeros((SUBLANES, n), F32)
    for bit, (qr, qi) in ((1, (p1r, p1i)), (2, (p2r, p2i)), (4, (p4r, p4i)), (8, (p8r, p8i))):
        nr, ni = _cmul(pwr, pwi, qr, qi)
        take = ((row + 1) & bit) != 0
        pwr = jnp.where(take, nr, pwr)
        pwi = jnp.where(take, ni, pwi)

    u = u_ref[0]
    hs_ref[...] = jnp.dot(u.astype(BF16), wb_ref[...], preferred_element_type=F32)

    def group(g, carry):
        cr, ci = carry
        r0 = pl.multiple_of(g * SUBLANES, SUBLANES)
        gr = hs_ref[pl.ds(r0, SUBLANES), 0:n]
        gi = hs_ref[pl.ds(r0, SUBLANES), n:2 * n]
        xr, xi = _cmul(fr, fi, gr, gi)
        for s, (qr, qi) in ((1, (p1r, p1i)), (2, (p2r, p2i)), (4, (p4r, p4i))):
            sr, si = _cmul(qr, qi, _shift_rows(xr, s, row, 0.0), _shift_rows(xi, s, row, 0.0))
            xr, xi = xr + sr, xi + si
        tr, ti = _cmul(pwr, pwi, cr, ci)
        hr, hi = xr + tr, xi + ti
        hs_ref[pl.ds(r0, SUBLANES), 0:n] = hr
        hs_ref[pl.ds(r0, SUBLANES), n:2 * n] = hi
        return hr[SUBLANES - 1:SUBLANES], hi[SUBLANES - 1:SUBLANES]

    cr, ci = lax.fori_loop(0, chunk // SUBLANES, group, (car_ref[:, 0:n], car_ref[:, n:2 * n]))
    car_ref[:, 0:n] = cr
    car_ref[:, n:2 * n] = ci

    y = jnp.dot(hs_ref[...].astype(BF16), wc_ref[...], preferred_element_type=F32) + d_ref[...] * u
    z = jax.nn.gelu(y)
    gate = jax.nn.sigmoid(jnp.dot(z.astype(BF16), wg_ref[...], preferred_element_type=F32) + bg_ref[...])
    z_ref[0] = z * gate

    @pl.when(c == pl.num_programs(1) - 1)
    def _():
        sr_ref[0] = hs_ref[last_row:last_row + 1, 0:n]
        si_ref[0] = hs_ref[last_row:last_row + 1, n:2 * n]


def _blockdiag(blocks):
    g, r, c = blocks.shape
    eye = jnp.eye(g, dtype=blocks.dtype)
    return (blocks[:, :, None, :] * eye[:, None, :, None]).reshape(g * r, g * c)


def _s5(u3, h0r, h0i, t_valid, lam_re, lam_im, log_dt, b_re, b_im, c_re, c_im, d_skip, w_glu, b_glu):
    b, t, _ = u3.shape
    chunk = min(256, t)
    last_row = (t_valid - 1) % chunk
    lam = jnp.concatenate([lam_re.reshape(1, SSM_N), lam_im.reshape(1, SSM_N),
                           jnp.broadcast_to(log_dt[:, None], (SSM_GROUPS, SSM_STATE)).reshape(1, SSM_N),
                           jnp.zeros((SUBLANES - 3, SSM_N), F32)], axis=0)
    wb = jnp.concatenate([_blockdiag(jnp.swapaxes(b_re, 1, 2)), _blockdiag(jnp.swapaxes(b_im, 1, 2))],
                         axis=1).astype(BF16)
    wc = jnp.concatenate([_blockdiag(jnp.swapaxes(c_re, 1, 2)), -_blockdiag(jnp.swapaxes(c_im, 1, 2))],
                         axis=0).astype(BF16)
    full = lambda a: pl.BlockSpec(a.shape, lambda bi, c: (0,) * a.ndim)
    seq = pl.BlockSpec((1, chunk, SSM_W), lambda bi, c: (bi, c, 0))
    st = pl.BlockSpec((1, 1, SSM_N), lambda bi, c: (bi, 0, 0))
    d2 = d_skip.reshape(1, SSM_W)
    wg = w_glu.astype(BF16)
    bg = b_glu.reshape(1, SSM_W)
    return pl.pallas_call(
        functools.partial(_s5_kernel, last_row=last_row),
        grid=(b, t // chunk),
        in_specs=[seq, st, st, full(lam), full(wb), full(wc), full(d2), full(wg), full(bg)],
        out_specs=(seq, st, st),
        out_shape=(jax.ShapeDtypeStruct((b, t, SSM_W), F32), jax.ShapeDtypeStruct((b, 1, SSM_N), F32),
                   jax.ShapeDtypeStruct((b, 1, SSM_N), F32)),
        scratch_shapes=[pltpu.VMEM((chunk, 2 * SSM_N), F32), pltpu.VMEM((1, 2 * SSM_N), F32)],
        compiler_params=_cparams(("arbitrary", "arbitrary"), 48),
        name="s5",
    )(u3, h0r.reshape(b, 1, SSM_N), h0i.reshape(b, 1, SSM_N), lam, wb, wc, d2, wg, bg)


def _lru_kernel(x_ref, xg_ref, buf_ref, h0_ref, cw_ref, vec_ref, wa_ref, wx_ref,
                y_ref, nb_ref, hl_ref, ext_ref, a_ref, b_ref, car_ref, *, last_row):
    c = pl.program_id(1)
    chunk = x_ref.shape[1]
    pad = SUBLANES

    @pl.when(c == 0)
    def _():
        ext_ref[0:pad, :] = jnp.zeros((pad, LRU_W), F32)
        ext_ref[pad - (CONV_W - 1):pad, :] = buf_ref[0]
        car_ref[...] = h0_ref[0]

    @pl.when(c > 0)
    def _():
        ext_ref[0:pad, :] = ext_ref[chunk:chunk + pad, :]

    ext_ref[pad:pad + chunk, :] = x_ref[0]
    xc = vec_ref[0:1, :] + cw_ref[CONV_W - 1:CONV_W, :] * x_ref[0]
    for jj in range(CONV_W - 1):
        xc = xc + cw_ref[jj:jj + 1, :] * ext_ref[pad - (CONV_W - 1) + jj:pad - (CONV_W - 1) + jj + chunk, :]
    xb = xc.astype(BF16)
    r = jax.nn.sigmoid(jnp.dot(xb, wa_ref[...], preferred_element_type=F32) + vec_ref[1:2, :])
    gi = jax.nn.sigmoid(jnp.dot(xb, wx_ref[...], preferred_element_type=F32) + vec_ref[2:3, :])
    lam = vec_ref[3:4, :]
    log_sig = -(jnp.maximum(-lam, 0.0) + jnp.log(1.0 + jnp.exp(-jnp.abs(lam))))
    log_a = LRU_C * r * log_sig
    a = jnp.exp(log_a)
    one_minus_a2 = -jnp.tanh(log_a) * (jnp.exp(2.0 * log_a) + 1.0)
    a_ref[...] = a
    b_ref[...] = jnp.sqrt(one_minus_a2) * (gi * xc)
    row = lax.broadcasted_iota(jnp.int32, (SUBLANES, 1), 0)

    def group(g, carry):
        r0 = pl.multiple_of(g * SUBLANES, SUBLANES)
        av = a_ref[pl.ds(r0, SUBLANES), :]
        bv = b_ref[pl.ds(r0, SUBLANES), :]
        for s in (1, 2, 4):
            bv = av * _shift_rows(bv, s, row, 0.0) + bv
            av = av * _shift_rows(av, s, row, 1.0)
        h = bv + av * carry
        b_ref[pl.ds(r0, SUBLANES), :] = h
        return h[SUBLANES - 1:SUBLANES]

    car_ref[...] = lax.fori_loop(0, chunk // SUBLANES, group, car_ref[...])
    y_ref[0] = b_ref[...] * jax.nn.gelu(xg_ref[0])

    @pl.when(c == pl.num_programs(1) - 1)
    def _():
        hl_ref[0] = b_ref[last_row:last_row + 1, :]
        lo = pad + last_row - (CONV_W - 2)
        nb_ref[0] = ext_ref[lo:lo + CONV_W - 1, :]


def _rglru(x3, xg3, conv_buf, h0, t_valid, conv_w, conv_b, w_a, b_a, w_x, b_x, lam):
    b, t, _ = x3.shape
    chunk = min(256, t)
    last_row = (t_valid - 1) % chunk
    vec = jnp.concatenate([conv_b[None], b_a[None], b_x[None], lam[None], jnp.zeros((SUBLANES - 4, LRU_W), F32)], 0)
    wa = _blockdiag(w_a).astype(BF16)
    wx = _blockdiag(w_x).astype(BF16)
    full = lambda a: pl.BlockSpec(a.shape, lambda bi, c: (0,) * a.ndim)
    seq = pl.BlockSpec((1, chunk, LRU_W), lambda bi, c: (bi, c, 0))
    return pl.pallas_call(
        functools.partial(_lru_kernel, last_row=last_row),
        grid=(b, t // chunk),
        in_specs=[seq, seq, pl.BlockSpec((1, CONV_W - 1, LRU_W), lambda bi, c: (bi, 0, 0)),
                  pl.BlockSpec((1, 1, LRU_W), lambda bi, c: (bi, 0, 0)), full(conv_w), full(vec), full(wa), full(wx)],
        out_specs=(seq, pl.BlockSpec((1, CONV_W - 1, LRU_W), lambda bi, c: (bi, 0, 0)),
                   pl.BlockSpec((1, 1, LRU_W), lambda bi, c: (bi, 0, 0))),
        out_shape=(jax.ShapeDtypeStruct((b, t, LRU_W), F32), jax.ShapeDtypeStruct((b, CONV_W - 1, LRU_W), F32),
                   jax.ShapeDtypeStruct((b, 1, LRU_W), F32)),
        scratch_shapes=[pltpu.VMEM((chunk + 2 * SUBLANES, LRU_W), F32), pltpu.VMEM((chunk, LRU_W), F32),
                        pltpu.VMEM((chunk, LRU_W), F32), pltpu.VMEM((1, LRU_W), F32)],
        compiler_params=_cparams(("arbitrary", "arbitrary"), 32),
        name="rglru",
    )(x3, xg3, conv_buf, h0.reshape(b, 1, LRU_W), conv_w, vec, wa, wx)


def _layer_norm(x, g, b):
    mu = jnp.mean(x, axis=-1, keepdims=True)
    xc = x - mu
    var = jnp.mean(xc * xc, axis=-1, keepdims=True)
    return xc * lax.rsqrt(var + LN_EPS) * g + b


def _mix_kernel(ya_ref, ys_ref, yl_ref, gl_ref, x_ref, wbr_ref, wout_ref, vec_ref, wr_ref, br_ref,
                x1_ref, x1b_ref, comb_ref, pos_ref, cnt_ref, *, alpha, tiles_per_block):
    i = pl.program_id(0)
    tm = x_ref.shape[0]
    mixed = None
    for n, y_ref in enumerate((ya_ref, ys_ref, yl_ref)):
        proj = jnp.dot(y_ref[...].astype(BF16), wbr_ref[n], preferred_element_type=F32)
        t = jax.nn.sigmoid(gl_ref[:, n * D_MODEL:(n + 1) * D_MODEL]) * proj
        mixed = t if mixed is None else mixed + t
    mixed = jnp.dot(mixed.astype(BF16), wout_ref[...], preferred_element_type=F32) + vec_ref[0:1, :]
    x1 = _layer_norm(alpha * x_ref[...] + mixed, vec_ref[1:2, :], vec_ref[2:3, :])
    x1_ref[...] = x1
    x1b_ref[...] = x1.astype(BF16)

    logits = jnp.dot(x1.astype(BF16), wr_ref[...].astype(BF16), preferred_element_type=F32) + br_ref[...]
    lane = lax.broadcasted_iota(jnp.int32, (tm, N_EXPERTS), 1)
    work = logits
    vals, hits = [], []
    for _ in range(TOP_K):
        v = jnp.max(work, axis=1, keepdims=True)
        ix = jnp.min(jnp.where(work == v, lane, N_EXPERTS), axis=1, keepdims=True)
        hit = lane == ix
        vals.append(v)
        hits.append(hit)
        work = jnp.where(hit, -jnp.inf, work)
    es = [jnp.exp(v - vals[0]) for v in vals]
    den = es[0] + es[1] + es[2] + es[3]
    comb = jnp.zeros((tm, N_EXPERTS), F32)
    routed = jnp.zeros((tm, N_EXPERTS), F32)
    for e, hit in zip(es, hits):
        comb = comb + jnp.where(hit, e / den, 0.0)
        routed = routed + jnp.where(hit, 1.0, 0.0)
    comb_ref[...] = comb

    @pl.when(i % tiles_per_block == 0)
    def _():
        cnt_ref[...] = jnp.zeros(cnt_ref.shape, F32)

    before = (lax.broadcasted_iota(jnp.int32, (tm, tm), 1) < lax.broadcasted_iota(jnp.int32, (tm, tm), 0))
    rank = jnp.dot(jnp.where(before, 1.0, 0.0).astype(BF16), routed.astype(BF16), preferred_element_type=F32)
    pos = rank + cnt_ref[...]
    pos_ref[...] = jnp.where(routed > 0.5, pos, -1.0)
    cnt_ref[...] = cnt_ref[...] + jnp.sum(routed, axis=0, keepdims=True)


def _mix(ya, ys, yl, gl, x, w_branch, w_out, b_out, ln_g, ln_b, w_r, b_r, alpha, moe_block):
    n = x.shape[0]
    tm = min(256, n)
    row = lambda wd: pl.BlockSpec((tm, wd), lambda i: (i, 0))
    full = lambda a: pl.BlockSpec(a.shape, lambda i: (0,) * a.ndim)
    vec = jnp.concatenate([b_out[None], ln_g[None], ln_b[None], jnp.zeros((SUBLANES - 3, D_MODEL), F32)], 0)
    wbr = w_branch.astype(BF16)
    wout = w_out.astype(BF16)
    br = b_r.reshape(1, N_EXPERTS)
    return pl.pallas_call(
        functools.partial(_mix_kernel, alpha=alpha, tiles_per_block=moe_block // tm),
        grid=(n // tm,),
        in_specs=[row(ATT_W), row(SSM_W), row(LRU_W), row(N_BRANCH * D_MODEL), row(D_MODEL),
                  full(wbr), full(wout), full(vec), full(w_r), full(br)],
        out_specs=(row(D_MODEL), row(D_MODEL), row(N_EXPERTS), row(N_EXPERTS)),
        out_shape=(jax.ShapeDtypeStruct((n, D_MODEL), F32), jax.ShapeDtypeStruct((n, D_MODEL), BF16),
                   jax.ShapeDtypeStruct((n, N_EXPERTS), F32), jax.ShapeDtypeStruct((n, N_EXPERTS), F32)),
        scratch_shapes=[pltpu.VMEM((1, N_EXPERTS), F32)],
        compiler_params=_cparams(("arbitrary",), 48),
        name="mix_ln1_router",
    )(ya, ys, yl, gl, x, wbr, wout, vec, w_r, br)


MOE_BLOCK = 2048
MOE_CHUNK = 256
MOE_TAIL = 64
MOE_SCATTER = 512


def _moe_kernel(cnt_ref, xb_ref, post_ref, pos_ref, comb_ref, wgu_ref, bgu_ref, wdn_ref, bdn_ref, o_ref,
                *, ch, ch_tail, sb):
    tb_i = pl.program_id(0)
    e = pl.program_id(1)
    tb = xb_ref.shape[0]

    @pl.when(e == 0)
    def _():
        o_ref[...] = jnp.zeros(o_ref.shape, F32)

    cnt = cnt_ref[tb_i * N_EXPERTS + e]
    full = cnt // ch
    rem = cnt % ch
    short = jnp.logical_and(rem > 0, rem <= ch_tail)
    widen = jnp.logical_and(short, full > 0)
    tail_only = jnp.logical_and(short, full == 0)
    n_main = full + jnp.where(jnp.logical_and(rem > 0, jnp.logical_not(short)), 1, 0) - jnp.where(widen, 1, 0)
    lane_e = lax.broadcasted_iota(jnp.int32, (1, N_EXPERTS), 1) == e
    slot_row = post_ref[0]

    def run_chunk(base, ch):
        pick = (lax.broadcasted_iota(jnp.int32, (ch, tb), 0) + base).astype(F32) == slot_row
        xc = jnp.dot(jnp.where(pick, 1.0, 0.0).astype(BF16), xb_ref[...], preferred_element_type=F32).astype(BF16)
        gu = jnp.dot(xc, wgu_ref[0], preferred_element_type=F32) + bgu_ref[0]
        gate = jnp.minimum(gu[:, :D_FF], SWIGLU_LIMIT)
        up = jnp.clip(gu[:, D_FF:], -SWIGLU_LIMIT, SWIGLU_LIMIT)
        act = (up + 1.0) * gate * jax.nn.sigmoid(SWIGLU_ALPHA * gate)
        y = (jnp.dot(act.astype(BF16), wdn_ref[0], preferred_element_type=F32) + bdn_ref[0]).astype(BF16)
        for s in range(tb // sb):
            rows = slice(s * sb, (s + 1) * sb)
            slot_col = jnp.sum(jnp.where(lane_e, pos_ref[rows, :], 0.0), axis=1, keepdims=True)
            w_col = jnp.sum(jnp.where(lane_e, comb_ref[rows, :], 0.0), axis=1, keepdims=True)
            upd = None
            for k0 in range(0, ch, MOE_CHUNK):
                kw = min(MOE_CHUNK, ch - k0)
                put = (lax.broadcasted_iota(jnp.int32, (sb, kw), 1) + (base + k0)).astype(F32) == slot_col
                t = jnp.dot(jnp.where(put, w_col, 0.0).astype(BF16), y[k0:k0 + kw], preferred_element_type=F32)
                upd = t if upd is None else upd + t
            o_ref[rows, :] += upd

    def main_body(c, carry):
        run_chunk(c * ch, ch)
        return carry

    lax.fori_loop(0, n_main, main_body, 0)

    @pl.when(widen)
    def _():
        run_chunk(n_main * ch, ch + ch_tail)

    @pl.when(tail_only)
    def _():
        run_chunk(0, ch_tail)


def _moe(x1b, comb, pos, wgu, bgu, wdn, bdn):
    n = x1b.shape[0]
    tb = min(MOE_BLOCK, n)
    ch = min(MOE_CHUNK, tb)
    ch_tail = min(MOE_TAIL, ch)
    sb = min(MOE_SCATTER, tb)
    n_tb = n // tb
    routed = (pos >= 0).astype(jnp.int32)
    cnt = routed.reshape(n_tb, tb, N_EXPERTS).sum(axis=1).reshape(-1)
    pos_t = pos.T.reshape(N_EXPERTS, 1, n)
    grid_spec = pltpu.PrefetchScalarGridSpec(
        num_scalar_prefetch=1,
        grid=(n_tb, N_EXPERTS),
        in_specs=[pl.BlockSpec((tb, D_MODEL), lambda t, e, c: (t, 0)),
                  pl.BlockSpec((1, 1, tb), lambda t, e, c: (e, 0, t)),
                  pl.BlockSpec((tb, N_EXPERTS), lambda t, e, c: (t, 0)),
                  pl.BlockSpec((tb, N_EXPERTS), lambda t, e, c: (t, 0)),
                  pl.BlockSpec((1, D_MODEL, 2 * D_FF), lambda t, e, c: (e, 0, 0)),
                  pl.BlockSpec((1, 1, 2 * D_FF), lambda t, e, c: (e, 0, 0)),
                  pl.BlockSpec((1, D_FF, D_MODEL), lambda t, e, c: (e, 0, 0)),
                  pl.BlockSpec((1, 1, D_MODEL), lambda t, e, c: (e, 0, 0))],
        out_specs=pl.BlockSpec((tb, D_MODEL), lambda t, e, c: (t, 0)),
    )
    return pl.pallas_call(
        functools.partial(_moe_kernel, ch=ch, ch_tail=ch_tail, sb=sb),
        grid_spec=grid_spec,
        out_shape=jax.ShapeDtypeStruct((n, D_MODEL), F32),
        compiler_params=_cparams(("arbitrary", "arbitrary"), 56),
        name="moe",
    )(cnt, x1b, pos_t, pos, comb, wgu, bgu, wdn, bdn)


def _ln2_kernel(x_ref, m_ref, vec_ref, o_ref, *, alpha):
    o_ref[...] = _layer_norm(alpha * x_ref[...] + m_ref[...], vec_ref[0:1, :], vec_ref[1:2, :])


def _ln2(x1, moe_out, g, b, alpha):
    n = x1.shape[0]
    tm = min(512, n)
    vec = jnp.concatenate([g[None], b[None], jnp.zeros((SUBLANES - 2, D_MODEL), F32)], 0)
    row = pl.BlockSpec((tm, D_MODEL), lambda i: (i, 0))
    return pl.pallas_call(
        functools.partial(_ln2_kernel, alpha=alpha),
        grid=(n // tm,),
        in_specs=[row, row, pl.BlockSpec(vec.shape, lambda i: (0, 0))],
        out_specs=row,
        out_shape=jax.ShapeDtypeStruct((n, D_MODEL), F32),
        compiler_params=_cparams(("arbitrary",), 32),
        name="ln2",
    )(x1, moe_out, vec)


def _pad_rows(a, rows):
    pad = [(0, 0)] * a.ndim
    pad[1] = (0, rows - a.shape[1])
    return jnp.pad(a, pad)


def kernel(x_prompt, x_sample, cache_k, cache_v, cache_kidx, state_ssm_re, state_ssm_im, state_lru_h, state_lru_conv, page_table, w_in, b_in, ssm_lam_re, ssm_lam_im, ssm_log_dt, ssm_b_re, ssm_b_im, ssm_c_re, ssm_c_im, ssm_d, ssm_w_glu, ssm_b_glu, lru_conv_w, lru_conv_b, lru_w_a, lru_b_a, lru_w_x, lru_b_x, lru_lam, w_branch, w_out, b_out, ln1_g, ln1_b, moe_w_r, moe_b_r, moe_w_gu, moe_b_gu, moe_w_dn, moe_b_dn, ln2_g, ln2_b):
    depth = w_in.shape[0]
    alpha = (2 * depth) ** 0.25
    bp, tp, _ = x_prompt.shape
    db, ns, _ = x_sample.shape
    assert ns <= SAMPLE_Q
    n_pages = page_table.shape[1]
    past = n_pages * PAGE_SIZE
    n_pool = cache_k.shape[1]
    pt_flat = page_table.reshape(-1).astype(jnp.int32)
    ck_lines = cache_k.reshape(depth, n_pool, KEY_LINES, HEAD_DIM)
    cv_lines = cache_v.reshape(depth, n_pool, KEY_LINES, HEAD_DIM)
    topk_p = min(TOPK_MAX, tp // 4)
    topk_s = min(TOPK_MAX, (past + ns) // 4)

    q_end = 3 * ATT_W
    i_end = q_end + IDX_KI + IDX_DIM + IDX_HEADS

    def regroup(a):
        padw = [(0, 0)] * (a.ndim - 1) + [(0, IDX_PAD - (i_end - q_end))]
        return jnp.concatenate([a[..., :q_end], jnp.pad(a[..., q_end:i_end], padw), a[..., i_end:]], axis=-1)

    w_in_r = regroup(w_in).astype(BF16)
    b_in_r = regroup(b_in).reshape(depth, 1, N_IN_PAD)
    wgu_b = moe_w_gu.astype(BF16)
    wdn_b = moe_w_dn.astype(BF16)
    bgu3 = moe_b_gu.reshape(depth, N_EXPERTS, 1, 2 * D_FF)
    bdn3 = moe_b_dn.reshape(depth, N_EXPERTS, 1, D_MODEL)

    xp = x_prompt.reshape(bp * tp, D_MODEL)
    xs = x_sample.reshape(db * ns, D_MODEL)
    new_p = [[] for _ in range(7)]
    new_s = [[] for _ in range(7)]

    def channel_mix(l, x, ya, ys, yl, gl):
        n = x.shape[0]
        x1, x1b, comb, pos = _mix(ya, ys, yl, gl, x, w_branch[l], w_out[l], b_out[l], ln1_g[l], ln1_b[l],
                                  moe_w_r[l], moe_b_r[l], alpha, min(MOE_BLOCK, n))
        moe_out = _moe(x1b, comb, pos, wgu_b[l], bgu3[l], wdn_b[l], bdn3[l])
        return _ln2(x1, moe_out, ln2_g[l], ln2_b[l], alpha)

    def key_lines(a):
        a = _pad_rows(a.reshape(db, ns, N_HEADS, HEAD_DIM), PAGE_SIZE)
        return a.reshape(db, KEY_LINES, HEAD_DIM).astype(BF16)

    for l in range(depth):
        ssm_p = (ssm_lam_re[l], ssm_lam_im[l], ssm_log_dt[l], ssm_b_re[l], ssm_b_im[l], ssm_c_re[l],
                 ssm_c_im[l], ssm_d[l], ssm_w_glu[l], ssm_b_glu[l])
        lru_p = (lru_conv_w[l], lru_conv_b[l], lru_w_a[l], lru_b_a[l], lru_w_x[l], lru_b_x[l], lru_lam[l])

        q_hm, k, k_hm, v, v_hm, idx, u, xr, xg, gl = _in_proj(xp, w_in_r[l], b_in_r[l])
        r3 = lambda a: a.reshape(bp, tp, a.shape[-1])
        idx3 = r3(idx)
        bias = _dsa_index(idx3, topk_p)
        ya = _dsa_attn(q_hm, k_hm, v_hm, bias).reshape(bp * tp, ATT_W)
        zeros_n = jnp.zeros((bp, SSM_N), F32)
        ys, s_re, s_im = _s5(r3(u), zeros_n, zeros_n, tp, *ssm_p)
        yl, nbuf, hl = _rglru(r3(xr), r3(xg), jnp.zeros((bp, CONV_W - 1, LRU_W), F32), jnp.zeros((bp, LRU_W), F32),
                              tp, *lru_p)
        st = (k.reshape(bp, tp, N_HEADS, HEAD_DIM), v.reshape(bp, tp, N_HEADS, HEAD_DIM),
              idx3[:, :, IDX_KI:IDX_KI + IDX_DIM], s_re.reshape(bp, SSM_GROUPS, SSM_STATE),
              s_im.reshape(bp, SSM_GROUPS, SSM_STATE), nbuf, hl.reshape(bp, LRU_W))
        for jj in range(7):
            new_p[jj].append(st[jj])
        xp = channel_mix(l, xp, ya, ys.reshape(bp * tp, SSM_W), yl.reshape(bp * tp, LRU_W), gl)

        q_hm, k, _, v, _, idx, u, xr, xg, gl = _in_proj(xs, w_in_r[l], b_in_r[l])
        r3 = lambda a: a.reshape(db, ns, a.shape[-1])
        idx3 = r3(idx)
        ki_new = idx3[:, :, IDX_KI:IDX_KI + IDX_DIM]
        keep = _sample_index(pt_flat, _pad_rows(idx3, SAMPLE_ROWS), _pad_rows(ki_new, PAGE_SIZE), cache_kidx, l,
                             n_pages, topk_s, ns)
        q_lines = _pad_rows(jnp.transpose(q_hm.reshape(N_HEADS, db, ns, HEAD_DIM), (1, 2, 0, 3)), SAMPLE_Q)
        ya = _sample_attn(pt_flat, q_lines.reshape(db, SAMPLE_LINES, HEAD_DIM), keep, key_lines(k), key_lines(v),
                          ck_lines, cv_lines, l, n_pages)
        ya = ya.reshape(db, SAMPLE_Q, ATT_W)[:, :ns].reshape(db * ns, ATT_W)
        ys, s_re, s_im = _s5(_pad_rows(r3(u), SAMPLE_ROWS), state_ssm_re[l].reshape(db, SSM_N),
                             state_ssm_im[l].reshape(db, SSM_N), ns, *ssm_p)
        yl, nbuf, hl = _rglru(_pad_rows(r3(xr), SAMPLE_ROWS), _pad_rows(r3(xg), SAMPLE_ROWS), state_lru_conv[l],
                              state_lru_h[l], ns, *lru_p)
        st = (k.reshape(db, ns, N_HEADS, HEAD_DIM), v.reshape(db, ns, N_HEADS, HEAD_DIM), ki_new,
              s_re.reshape(db, SSM_GROUPS, SSM_STATE), s_im.reshape(db, SSM_GROUPS, SSM_STATE), nbuf,
              hl.reshape(db, LRU_W))
        for jj in range(7):
            new_s[jj].append(st[jj])
        xs = channel_mix(l, xs, ya, ys[:, :ns].reshape(db * ns, SSM_W), yl[:, :ns].reshape(db * ns, LRU_W), gl)

    k_p, v_p, ki_p, sre_p, sim_p, lc_p, lh_p = [jnp.stack(a) for a in new_p]
    k_s, v_s, ki_s, sre_s, sim_s, lc_s, lh_s = [jnp.stack(a) for a in new_s]
    return (xp.reshape(bp, tp, D_MODEL), xs.reshape(db, ns, D_MODEL), k_p, v_p, ki_p, sre_p, sim_p, lh_p, lc_p,
            k_s, v_s, ki_s, sre_s, sim_s, lh_s, lc_s)
```

```python
import functools
import math

import jax
import jax.numpy as jnp
from jax import lax
from jax.experimental import pallas as pl
from jax.experimental.pallas import tpu as pltpu

F32 = jnp.float32
BF16 = jnp.bfloat16

D_MODEL = 1024
N_HEADS = 8
HEAD_DIM = 64
ATT_W = N_HEADS * HEAD_DIM
IDX_HEADS = 4
IDX_DIM = 64
TOPK_MAX = 256
PAGE_SIZE = 128
SSM_W = 512
SSM_GROUPS = 32
SSM_GROUP = 16
SSM_STATE = 64
SSM_N = SSM_GROUPS * SSM_STATE
LRU_W = 512
CONV_W = 4
LRU_C = 8.0
N_EXPERTS = 32
TOP_K = 4
D_FF = 1024
SWIGLU_LIMIT = 7.0
SWIGLU_ALPHA = 1.702
N_BRANCH = 3
LN_EPS = 1e-5
IDX_PAD = 384
IDX_KI = IDX_HEADS * IDX_DIM
IDX_WI = IDX_KI + IDX_DIM
NEG_BIG = -1e30
M_INIT = -1e29
QK_SCALE = HEAD_DIM ** -0.5 * math.log2(math.e)
SUBLANES = 8
LANES = 128


def _cparams(sem, vmem_mb):
    return pltpu.CompilerParams(dimension_semantics=sem, vmem_limit_bytes=vmem_mb * 1024 * 1024)


_SEG_W = (ATT_W, ATT_W, ATT_W, IDX_PAD, SSM_W, LRU_W, LRU_W, N_BRANCH * D_MODEL)
_SEG_OFF = tuple(sum(_SEG_W[:i]) for i in range(len(_SEG_W)))
N_IN_PAD = sum(_SEG_W)


def _in_proj_kernel(x_ref, w_ref, b_ref, wvt_ref, bvt_ref, q_ref, k_ref, kb_ref, v_ref, vt_ref, idx_ref, u_ref,
                    xr_ref, xg_ref, gl_ref):
    xb = x_ref[...].astype(BF16)

    def seg(i):
        off, wd = _SEG_OFF[i], _SEG_W[i]
        return jnp.dot(xb, w_ref[:, off:off + wd], preferred_element_type=F32) + b_ref[:, off:off + wd]

    tm = x_ref.shape[0]
    q = seg(0) * QK_SCALE
    k = seg(1)
    v = seg(2)
    k_ref[...] = k
    v_ref[...] = v
    vt = lax.dot_general(wvt_ref[...], xb, (((1,), (1,)), ((), ())), preferred_element_type=F32) + bvt_ref[...]
    ones = jnp.ones((HEAD_DIM, tm), F32)
    for h in range(N_HEADS):
        sl = slice(h * HEAD_DIM, (h + 1) * HEAD_DIM)
        q_ref[h] = q[:, sl].astype(BF16)
        kb_ref[h] = k[:, sl].astype(BF16)
        vt_ref[h] = jnp.concatenate([vt[sl, :], ones], axis=0).astype(BF16)
    idx_ref[...] = seg(3)
    u_ref[...] = seg(4)
    xr_ref[...] = seg(5)
    xg_ref[...] = seg(6)
    gl_ref[...] = seg(7)


def _in_proj(x, w, b):
    n = x.shape[0]
    tm = min(256, n)
    row = lambda wd: pl.BlockSpec((tm, wd), lambda i: (i, 0))
    full = lambda a: pl.BlockSpec(a.shape, lambda i: (0,) * a.ndim)
    heads = lambda wd: pl.BlockSpec((N_HEADS, tm, wd), lambda i: (0, i, 0))
    v_off = _SEG_OFF[2]
    wvt = w[:, v_off:v_off + ATT_W].T
    bvt = b[:, v_off:v_off + ATT_W].T
    out_shape = (
        jax.ShapeDtypeStruct((N_HEADS, n, HEAD_DIM), BF16),
        jax.ShapeDtypeStruct((n, ATT_W), F32),
        jax.ShapeDtypeStruct((N_HEADS, n, HEAD_DIM), BF16),
        jax.ShapeDtypeStruct((n, ATT_W), F32),
        jax.ShapeDtypeStruct((N_HEADS, 2 * HEAD_DIM, n), BF16),
        jax.ShapeDtypeStruct((n, IDX_PAD), F32),
        jax.ShapeDtypeStruct((n, SSM_W), F32),
        jax.ShapeDtypeStruct((n, LRU_W), F32),
        jax.ShapeDtypeStruct((n, LRU_W), F32),
        jax.ShapeDtypeStruct((n, N_BRANCH * D_MODEL), F32),
    )
    return pl.pallas_call(
        _in_proj_kernel,
        grid=(n // tm,),
        in_specs=[row(D_MODEL), full(w), full(b), full(wvt), full(bvt)],
        out_specs=tuple(pl.BlockSpec((N_HEADS, 2 * HEAD_DIM, tm), lambda i: (0, 0, i)) if s.shape[2:] == (n,)
                        else heads(s.shape[2]) if len(s.shape) == 3 else row(s.shape[1]) for s in out_shape),
        out_shape=out_shape,
        compiler_params=_cparams(("arbitrary",), 56),
        name="in_proj",
    )(x, w, b, wvt, bvt)


BISECT_STEPS = 20


def _prefix_matrix(ck):
    r = lax.broadcasted_iota(jnp.int32, (ck, ck), 0)
    c = lax.broadcasted_iota(jnp.int32, (ck, ck), 1)
    return jnp.where(r <= c, 1.0, 0.0).astype(BF16)


def _select_rows(s_ref, rows, rb, n_chunks, ck, kk, n_adm):
    nl = ck // LANES

    def tile(c, j):
        return s_ref[rows, pl.ds(pl.multiple_of(c * ck + j * LANES, LANES), LANES)]

    def lanes(x):
        return jnp.broadcast_to(x, (rb, LANES))

    def count_gt(x):
        xb = lanes(x)

        def body(c, acc):
            for j in range(nl):
                acc = acc + lax.shift_right_arithmetic(pltpu.bitcast(xb - tile(c, j), jnp.int32), 31)
            return acc
        neg = lax.fori_loop(0, n_chunks, body, jnp.zeros((rb, LANES), jnp.int32))
        return -jnp.sum(neg.astype(F32), axis=1, keepdims=True)

    def min_above_max_upto(lo, hi):
        lob, hib = lanes(lo), lanes(hi)

        def body(c, acc):
            mn, mx = acc
            for j in range(nl):
                v = tile(c, j)
                mn = jnp.minimum(mn, jnp.where(v > lob, v, jnp.inf))
                mx = jnp.maximum(mx, jnp.where(v <= hib, v, -jnp.inf))
            return mn, mx
        mn, mx = lax.fori_loop(0, n_chunks, body, (jnp.full((rb, LANES), jnp.inf, F32),
                                                   jnp.full((rb, LANES), -jnp.inf, F32)))
        return jnp.min(mn, axis=1, keepdims=True), jnp.max(mx, axis=1, keepdims=True)

    def bisect_at(x, st):
        lo, hi, clo, chi = st
        c = count_gt(x)
        ge = c >= kk
        return (jnp.where(ge, x, lo), jnp.where(ge, hi, x), jnp.where(ge, c, clo), jnp.where(ge, chi, c))

    inf = jnp.full((rb, 1), jnp.inf, F32)
    row_min, row_max = min_above_max_upto(-inf, inf)
    st = (row_min - 1.0 - jnp.abs(row_min), row_max, n_adm, jnp.zeros((rb, 1), F32))
    st = lax.fori_loop(0, BISECT_STEPS, lambda _, s: bisect_at(0.5 * s[0] + 0.5 * s[1], s), st)

    def refine_cond(carry):
        return carry[3] > 0.5

    def refine(carry):
        st, a, b, _ = carry
        x = 0.5 * a + 0.5 * b
        st = bisect_at(jnp.where(x >= b, a, x), st)
        a, b = min_above_max_upto(st[0], st[1])
        return st, a, b, jnp.max(jnp.where(a == b, 0.0, 1.0))

    a, b = min_above_max_upto(st[0], st[1])
    st, tau, _, _ = lax.while_loop(refine_cond, refine, (st, a, b, jnp.max(jnp.where(a == b, 0.0, 1.0))))

    return tau, kk - st[3]


def _emit_selection(s_ref, n_chunks, ck, tau, need, emit):
    rows = s_ref.shape[0]
    tri = _prefix_matrix(ck)

    def body(c, seen):
        v = s_ref[:, pl.ds(pl.multiple_of(c * ck, LANES), ck)]
        eq = v == tau
        pref = jnp.dot(jnp.where(eq, 1.0, 0.0).astype(BF16), tri, preferred_element_type=F32) + seen
        emit(c, (v > tau) | (eq & (pref <= need)))
        return pref[:, ck - 1:ck]

    lax.fori_loop(0, n_chunks, body, jnp.zeros((rows, 1), F32))


def _indexer_scores(qi, wi, keys):
    sc = None
    for h in range(IDX_HEADS):
        d = lax.dot_general(qi[:, h * IDX_DIM:(h + 1) * IDX_DIM], keys, (((1,), (1,)), ((), ())),
                            preferred_element_type=F32)
        t = wi[:, h:h + 1] * jnp.maximum(d, 0.0)
        sc = t if sc is None else sc + t
    return sc


DSA_TQ = 256
DSA_CK = 512
DSA_RB = 64


def _dsa_index_kernel(q_ref, kw_ref, bias_ref, s_ref, tau_ref, need_ref, *, topk):
    tq, t_len = s_ref.shape
    ck = min(DSA_CK, t_len)
    rb = min(DSA_RB, tq)
    i = pl.program_id(1)
    t0 = i * tq
    n_valid = ((i + 1) * tq + ck - 1) // ck
    qi = q_ref[0, :, 0:IDX_KI].astype(BF16)
    wi = q_ref[0, :, IDX_WI:IDX_WI + IDX_HEADS]
    t_ids = t0 + lax.broadcasted_iota(jnp.int32, (tq, 1), 0)
    lane = lax.broadcasted_iota(jnp.int32, (1, ck), 1)

    def score_body(c, carry):
        off = pl.multiple_of(c * ck, LANES)
        keys = kw_ref[0, pl.ds(off, ck), 0:IDX_DIM].astype(BF16)
        sc = _indexer_scores(qi, wi, keys)
        s_ref[:, pl.ds(off, ck)] = jnp.where(lane + c * ck <= t_ids, sc, -jnp.inf)
        return carry

    lax.fori_loop(0, n_valid, score_body, 0)

    pair = (t_len // ck) % 2 == 0
    if pair:
        @pl.when(n_valid % 2 == 1)
        def _():
            s_ref[:, pl.ds(pl.multiple_of(n_valid * ck, LANES), ck)] = jnp.full((tq, ck), -jnp.inf, F32)
    n_search, ck_search = ((n_valid + 1) // 2, 2 * ck) if pair else (n_valid, ck)

    def block_body(bi, carry):
        r0 = pl.multiple_of(bi * rb, rb)
        rows = pl.ds(r0, rb)
        n_adm = (t0 + r0 + 1 + lax.broadcasted_iota(jnp.int32, (rb, 1), 0)).astype(F32)
        tau, need = _select_rows(s_ref, rows, rb, n_search, ck_search, jnp.minimum(n_adm, float(topk)), n_adm)
        tau_ref[rows, :] = tau
        need_ref[rows, :] = need
        return carry

    lax.fori_loop(0, tq // rb, block_body, 0)

    def emit(c, keep):
        bias_ref[0, :, pl.ds(pl.multiple_of(c * ck, LANES), ck)] = jnp.where(keep, 0.0, NEG_BIG).astype(BF16)

    _emit_selection(s_ref, n_valid, ck, tau_ref[...], need_ref[...], emit)

    def fill_body(c, carry):
        bias_ref[0, :, pl.ds(pl.multiple_of(c * ck, LANES), ck)] = jnp.full((tq, ck), NEG_BIG, BF16)
        return carry

    lax.fori_loop(n_valid, t_len // ck, fill_body, 0)


def _dsa_index(idx3, topk):
    b, t, _ = idx3.shape
    tq = min(DSA_TQ, t)
    return pl.pallas_call(
        functools.partial(_dsa_index_kernel, topk=topk),
        grid=(b, t // tq),
        in_specs=[pl.BlockSpec((1, tq, IDX_PAD), lambda bi, i: (bi, i, 0)),
                  pl.BlockSpec((1, t, LANES), lambda bi, i: (bi, 0, IDX_KI // LANES))],
        out_specs=pl.BlockSpec((1, tq, t), lambda bi, i: (bi, i, 0)),
        out_shape=jax.ShapeDtypeStruct((b, t, t), BF16),
        scratch_shapes=[pltpu.VMEM((tq, t), F32), pltpu.VMEM((tq, 1), F32), pltpu.VMEM((tq, 1), F32)],
        compiler_params=_cparams(("arbitrary", "arbitrary"), 48),
        name="dsa_index",
    )(idx3, idx3)


def _dsa_attn_kernel(q_ref, k_ref, v_ref, bias_ref, o_ref, m_ref, acc_ref, bias_scr):
    tq = q_ref.shape[1]
    tk = k_ref.shape[1]
    i = pl.program_id(1)
    j = pl.program_id(2)
    last = ((i + 1) * tq - 1) // tk

    @pl.when(j == 0)
    def _():
        m_ref[...] = jnp.full(m_ref.shape, M_INIT, F32)
        acc_ref[...] = jnp.zeros(acc_ref.shape, F32)

    @pl.when(j <= last)
    def _():
        bias_scr[...] = bias_ref[0].astype(F32)
        for h in range(N_HEADS):
            s = lax.dot_general(q_ref[h], k_ref[h], (((1,), (1,)), ((), ())), preferred_element_type=F32)
            s = s + bias_scr[...]
            m_prev = m_ref[h]
            m_new = jnp.maximum(m_prev, jnp.max(s, axis=1, keepdims=True))
            p = jnp.exp2(s - m_new)
            acc_ref[h] = jnp.exp2(m_prev - m_new) * acc_ref[h] + jnp.dot(p.astype(BF16), v_ref[h],
                                                                         preferred_element_type=F32)
            m_ref[h] = m_new

    @pl.when(j == pl.num_programs(2) - 1)
    def _():
        for h in range(N_HEADS):
            acc = acc_ref[h]
            o_ref[0, :, h * HEAD_DIM:(h + 1) * HEAD_DIM] = acc[:, 0:HEAD_DIM] / acc[:, HEAD_DIM:HEAD_DIM + 1]


def _dsa_attn(q_hm, k_hm, v_hm, bias):
    b, t, _ = bias.shape
    tq = min(DSA_TQ, t)
    tk = min(DSA_CK, t)
    nq, nk = t // tq, t // tk
    last = lambda i: ((i + 1) * tq - 1) // tk
    kv_idx = lambda bi, i, j: (0, bi * nk + jnp.minimum(j, last(i)), 0)
    return pl.pallas_call(
        _dsa_attn_kernel,
        grid=(b, nq, nk),
        in_specs=[pl.BlockSpec((N_HEADS, tq, HEAD_DIM), lambda bi, i, j: (0, bi * nq + i, 0)),
                  pl.BlockSpec((N_HEADS, tk, HEAD_DIM), kv_idx),
                  pl.BlockSpec((N_HEADS, tk, 2 * HEAD_DIM), kv_idx),
                  pl.BlockSpec((1, tq, tk), lambda bi, i, j: (bi, i, jnp.minimum(j, last(i))))],
        out_specs=pl.BlockSpec((1, tq, ATT_W), lambda bi, i, j: (bi, i, 0)),
        out_shape=jax.ShapeDtypeStruct((b, t, ATT_W), F32),
        scratch_shapes=[pltpu.VMEM((N_HEADS, tq, 1), F32), pltpu.VMEM((N_HEADS, tq, 2 * HEAD_DIM), F32),
                        pltpu.VMEM((tq, tk), F32)],
        compiler_params=_cparams(("arbitrary", "arbitrary", "arbitrary"), 32),
        name="dsa_attn",
    )(q_hm, k_hm, v_hm, bias)


N_ACC = 4
ATT_TQ = 512
ATT_TK = 1024


def _select_cols(s_ref, cols, n_chunks, ck, kk, n_adm):
    groups = ck // SUBLANES

    def bc(x):
        return jnp.broadcast_to(x, (SUBLANES, LANES))

    def reduce_keys(fn, op, init, dtype):
        def body(c, accs):
            accs = list(accs)
            slab = s_ref.at[pl.ds(pl.multiple_of(c * ck, ck), ck), cols]
            for g in range(groups):
                accs[g % N_ACC] = op(accs[g % N_ACC], fn(slab[g * SUBLANES:(g + 1) * SUBLANES, :]))
            return tuple(accs)
        accs = lax.fori_loop(0, n_chunks, body, tuple(jnp.full((SUBLANES, LANES), init, dtype) for _ in range(N_ACC)))
        out = accs[0]
        for a in accs[1:]:
            out = op(out, a)
        return out

    def count_gt(x):
        xb = bc(x)
        neg = reduce_keys(lambda v: lax.shift_right_arithmetic(pltpu.bitcast(xb - v, jnp.int32), 31), jnp.add, 0,
                          jnp.int32)
        return -jnp.sum(neg.astype(F32), axis=0, keepdims=True)

    def min_above(lo):
        lob = bc(lo)
        return jnp.min(reduce_keys(lambda v: jnp.where(v > lob, v, jnp.inf), jnp.minimum, jnp.inf, F32),
                       axis=0, keepdims=True)

    def max_upto(hi):
        hib = bc(hi)
        return jnp.max(reduce_keys(lambda v: jnp.where(v <= hib, v, -jnp.inf), jnp.maximum, -jnp.inf, F32),
                       axis=0, keepdims=True)

    def bisect_at(x, st):
        lo, hi, clo, chi = st
        c = count_gt(x)
        ge = c >= kk
        return (jnp.where(ge, x, lo), jnp.where(ge, hi, x), jnp.where(ge, c, clo), jnp.where(ge, chi, c))

    inf = jnp.full((1, LANES), jnp.inf, F32)
    col_min, col_max = min_above(-inf), max_upto(inf)
    st = (col_min - 1.0 - jnp.abs(col_min), col_max, n_adm, jnp.zeros((1, LANES), F32))
    st = lax.fori_loop(0, BISECT_STEPS, lambda _, s: bisect_at(0.5 * s[0] + 0.5 * s[1], s), st)

    def refine(carry):
        st, a, b, _ = carry
        x = 0.5 * a + 0.5 * b
        st = bisect_at(jnp.where(x >= b, a, x), st)
        a, b = min_above(st[0]), max_upto(st[1])
        return st, a, b, jnp.max(jnp.where(a == b, 0.0, 1.0))

    a, b = min_above(st[0]), max_upto(st[1])
    st, tau, _, _ = lax.while_loop(lambda carry: carry[3] > 0.5, refine,
                                   (st, a, b, jnp.max(jnp.where(a == b, 0.0, 1.0))))
    return tau, kk - st[3]


def _dsa_index_kernel_t(q_ref, wi_ref, kw_ref, bias_ref, s_ref, tau_ref, need_ref, *, topk):
    t_len, tq = s_ref.shape
    ck = min(DSA_CK, t_len)
    i = pl.program_id(1)
    t0 = i * tq
    n_valid = ((i + 1) * tq + ck - 1) // ck
    qi = q_ref[0, :, 0:IDX_KI].astype(BF16)
    wi = wi_ref[0]
    t_ids = t0 + lax.broadcasted_iota(jnp.int32, (1, tq), 1)
    key_row = lax.broadcasted_iota(jnp.int32, (ck, 1), 0)

    def score_body(c, carry):
        off = pl.multiple_of(c * ck, ck)
        keys = kw_ref[0, pl.ds(off, ck), 0:IDX_DIM].astype(BF16)
        sc = None
        for h in range(IDX_HEADS):
            d = lax.dot_general(keys, qi[:, h * IDX_DIM:(h + 1) * IDX_DIM], (((1,), (1,)), ((), ())),
                                preferred_element_type=F32)
            t = wi[h:h + 1, :] * jnp.maximum(d, 0.0)
            sc = t if sc is None else sc + t
        s_ref[pl.ds(off, ck), :] = jnp.where(key_row + c * ck <= t_ids, sc, -jnp.inf)
        return carry

    lax.fori_loop(0, n_valid, score_body, 0)

    for lb in range(tq // LANES):
        cols = slice(lb * LANES, (lb + 1) * LANES)
        n_adm = (t0 + lb * LANES + 1 + lax.broadcasted_iota(jnp.int32, (1, LANES), 1)).astype(F32)
        tau, need = _select_cols(s_ref, cols, n_valid, ck, jnp.minimum(n_adm, float(topk)), n_adm)
        tau_ref[:, cols] = tau
        need_ref[:, cols] = need

    tau = tau_ref[...]
    need = need_ref[...]
    before = (lax.broadcasted_iota(jnp.int32, (ck, ck), 1) <= lax.broadcasted_iota(jnp.int32, (ck, ck), 0))
    tri = jnp.where(before, 1.0, 0.0).astype(BF16)

    def emit_body(c, seen):
        off = pl.multiple_of(c * ck, ck)
        v = s_ref[pl.ds(off, ck), :]
        eq = v == tau
        pref = jnp.dot(tri, jnp.where(eq, 1.0, 0.0).astype(BF16), preferred_element_type=F32) + seen
        keep = (v > tau) | (eq & (pref <= need))
        bias_ref[0, pl.ds(off, ck), :] = jnp.where(keep, 0.0, NEG_BIG).astype(BF16)
        return pref[ck - 1:ck, :]

    lax.fori_loop(0, n_valid, emit_body, jnp.zeros((1, tq), F32))

    def fill_body(c, carry):
        bias_ref[0, pl.ds(pl.multiple_of(c * ck, ck), ck), :] = jnp.full((ck, tq), NEG_BIG, BF16)
        return carry

    lax.fori_loop(n_valid, t_len // ck, fill_body, 0)


def _dsa_index_t(idx3, wi_t, topk):
    b, t, _ = idx3.shape
    tq = min(DSA_TQ, t)
    return pl.pallas_call(
        functools.partial(_dsa_index_kernel_t, topk=topk),
        grid=(b, t // tq),
        in_specs=[pl.BlockSpec((1, tq, IDX_PAD), lambda bi, i: (bi, i, 0)),
                  pl.BlockSpec((1, SUBLANES, tq), lambda bi, i: (bi, 0, i)),
                  pl.BlockSpec((1, t, LANES), lambda bi, i: (bi, 0, IDX_KI // LANES))],
        out_specs=pl.BlockSpec((1, t, tq), lambda bi, i: (bi, 0, i)),
        out_shape=jax.ShapeDtypeStruct((b, t, t), BF16),
        scratch_shapes=[pltpu.VMEM((t, tq), F32), pltpu.VMEM((1, tq), F32), pltpu.VMEM((1, tq), F32)],
        compiler_params=_cparams(("arbitrary", "arbitrary"), 48),
        name="dsa_index",
    )(idx3, wi_t, idx3)


def _dsa_attn_kernel_t(q_ref, k_ref, vt_ref, bias_ref, o_ref, bias_scr, *state):
    m_refs, acc_refs = state[:N_HEADS], state[N_HEADS:]
    tq = q_ref.shape[1]
    tk = k_ref.shape[1]
    i = pl.program_id(1)
    j = pl.program_id(2)
    last = ((i + 1) * tq - 1) // tk

    @pl.when(j == 0)
    def _():
        for h in range(N_HEADS):
            m_refs[h][...] = jnp.full(m_refs[h].shape, M_INIT, F32)
            acc_refs[h][...] = jnp.zeros(acc_refs[h].shape, F32)

    @pl.when(j <= last)
    def _():
        bias = bias_ref[0].astype(F32)

        def qk(h):
            return lax.dot_general(k_ref[h], q_ref[h], (((1,), (1,)), ((), ())), preferred_element_type=F32)

        s_next = qk(0)
        for h in range(N_HEADS):
            s = s_next + bias
            if h + 1 < N_HEADS:
                s_next = qk(h + 1)
            m_prev = m_refs[h][...]
            m_new = jnp.maximum(m_prev, jnp.max(s, axis=0, keepdims=True))
            p = jnp.exp2(s - m_new)
            acc_refs[h][...] = (jnp.exp2(m_prev - m_new) * acc_refs[h][...]
                                + jnp.dot(vt_ref[h], p.astype(BF16), preferred_element_type=F32))
            m_refs[h][...] = m_new

    @pl.when(j == pl.num_programs(2) - 1)
    def _():
        for h in range(N_HEADS):
            acc = acc_refs[h][...]
            o_ref[0, h * HEAD_DIM:(h + 1) * HEAD_DIM, :] = acc[0:HEAD_DIM, :] / acc[HEAD_DIM:HEAD_DIM + 1, :]


def _dsa_attn_t(q_hm, k_hm, vt_hm, bias_t):
    b, t, _ = bias_t.shape
    tq = min(ATT_TQ, t)
    tk = min(ATT_TK, t)
    nq, nk = t // tq, t // tk
    last = lambda i: ((i + 1) * tq - 1) // tk
    return pl.pallas_call(
        _dsa_attn_kernel_t,
        grid=(b, nq, nk),
        in_specs=[pl.BlockSpec((N_HEADS, tq, HEAD_DIM), lambda bi, i, j: (0, bi * nq + i, 0)),
                  pl.BlockSpec((N_HEADS, tk, HEAD_DIM), lambda bi, i, j: (0, bi * nk + jnp.minimum(j, last(i)), 0)),
                  pl.BlockSpec((N_HEADS, 2 * HEAD_DIM, tk), lambda bi, i, j: (0, 0, bi * nk + jnp.minimum(j, last(i)))),
                  pl.BlockSpec((1, tk, tq), lambda bi, i, j: (bi, jnp.minimum(j, last(i)), i))],
        out_specs=pl.BlockSpec((1, ATT_W, tq), lambda bi, i, j: (bi, 0, i)),
        out_shape=jax.ShapeDtypeStruct((b, ATT_W, t), F32),
        scratch_shapes=[pltpu.VMEM((tk, tq), F32)] + [pltpu.VMEM((1, tq), F32)] * N_HEADS
                       + [pltpu.VMEM((2 * HEAD_DIM, tq), F32)] * N_HEADS,
        compiler_params=_cparams(("arbitrary", "arbitrary", "arbitrary"), 32),
        name="dsa_attn",
    )(q_hm, k_hm, vt_hm, bias_t)


PAGES_PER_STEP = 8
SAMPLE_ROWS = 8
SAMPLE_CK = 640


def _sample_index_kernel(pt_ref, q_ref, knew_ref, *refs, topk, n_new):
    pages = refs[:PAGES_PER_STEP]
    keep_ref, s_ref = refs[PAGES_PER_STEP:]
    j = pl.program_id(1)
    width = s_ref.shape[1]
    past = width - PAGE_SIZE
    step_w = PAGES_PER_STEP * PAGE_SIZE
    q = q_ref[0]
    wi = q[:, IDX_WI:IDX_WI + IDX_HEADS]
    q4 = jnp.concatenate([q[:, h * IDX_DIM:(h + 1) * IDX_DIM] for h in range(IDX_HEADS)], axis=0).astype(BF16)

    def scores(keys):
        d = lax.dot_general(q4, keys, (((1,), (1,)), ((), ())), preferred_element_type=F32)
        sc = None
        for h in range(IDX_HEADS):
            t = wi[:, h:h + 1] * jnp.maximum(d[h * SAMPLE_ROWS:(h + 1) * SAMPLE_ROWS], 0.0)
            sc = t if sc is None else sc + t
        return sc

    keys = jnp.concatenate([p[...] for p in pages], axis=0).astype(BF16)
    s_ref[:, pl.ds(pl.multiple_of(j * step_w, LANES), step_w)] = scores(keys)

    @pl.when(j == pl.num_programs(1) - 1)
    def _():
        sc = scores(knew_ref[0].astype(BF16))
        row = lax.broadcasted_iota(jnp.int32, (SAMPLE_ROWS, PAGE_SIZE), 0)
        lane = lax.broadcasted_iota(jnp.int32, (SAMPLE_ROWS, PAGE_SIZE), 1)
        ok = (lane <= row) & (lane < n_new)
        s_ref[:, past:past + PAGE_SIZE] = jnp.where(ok, sc, -jnp.inf)
        r1 = lax.broadcasted_iota(jnp.int32, (SAMPLE_ROWS, 1), 0)
        n_adm = (past + jnp.minimum(r1 + 1, n_new)).astype(F32)

        def emit(c, keep):
            keep_ref[0, :, pl.ds(pl.multiple_of(c * SAMPLE_CK, LANES), SAMPLE_CK)] = jnp.where(keep, 1.0, 0.0).astype(BF16)

        tau, need = _select_rows(s_ref, pl.ds(0, SAMPLE_ROWS), SAMPLE_ROWS, width // SAMPLE_CK, SAMPLE_CK,
                                 jnp.full((SAMPLE_ROWS, 1), float(topk), F32), n_adm)
        _emit_selection(s_ref, width // SAMPLE_CK, SAMPLE_CK, tau, need, emit)


def _page_specs(layer, n_pages, block):
    def spec(r):
        def index_map(b, j, pt):
            return (layer, pt[b * n_pages + j * PAGES_PER_STEP + r]) + (0,) * len(block)
        return pl.BlockSpec((None, None) + tuple(block), index_map)

    return [spec(r) for r in range(PAGES_PER_STEP)]


def _sample_index(pt_flat, idx_pad, knew_pad, cache_kidx, layer, n_pages, topk, n_new):
    db = idx_pad.shape[0]
    width = n_pages * PAGE_SIZE + PAGE_SIZE
    assert width % SAMPLE_CK == 0 and n_pages % PAGES_PER_STEP == 0
    grid_spec = pltpu.PrefetchScalarGridSpec(
        num_scalar_prefetch=1,
        grid=(db, n_pages // PAGES_PER_STEP),
        in_specs=[pl.BlockSpec((1, SAMPLE_ROWS, IDX_PAD), lambda b, j, pt: (b, 0, 0)),
                  pl.BlockSpec((1, PAGE_SIZE, IDX_DIM), lambda b, j, pt: (b, 0, 0))]
                 + _page_specs(layer, n_pages, (PAGE_SIZE, IDX_DIM)),
        out_specs=pl.BlockSpec((1, SAMPLE_ROWS, width), lambda b, j, pt: (b, 0, 0)),
        scratch_shapes=[pltpu.VMEM((SAMPLE_ROWS, width), F32)],
    )
    return pl.pallas_call(
        functools.partial(_sample_index_kernel, topk=topk, n_new=n_new),
        grid_spec=grid_spec,
        out_shape=jax.ShapeDtypeStruct((db, SAMPLE_ROWS, width), BF16),
        compiler_params=_cparams(("arbitrary", "arbitrary"), 32),
        name="sample_index",
    )(pt_flat, idx_pad, knew_pad, *([cache_kidx] * PAGES_PER_STEP))


SAMPLE_Q = 4
SAMPLE_LINES = SAMPLE_Q * N_HEADS
KEY_LINES = PAGE_SIZE * N_HEADS


def _sample_attn_kernel(pt_ref, q_ref, keep_ref, keepn_ref, knew_ref, vnew_ref, *refs):
    kp = refs[:PAGES_PER_STEP]
    vp = refs[PAGES_PER_STEP:2 * PAGES_PER_STEP]
    o_ref, m_ref, l_ref, acc_ref = refs[2 * PAGES_PER_STEP:]
    j = pl.program_id(1)

    @pl.when(j == 0)
    def _():
        m_ref[...] = jnp.full(m_ref.shape, M_INIT, F32)
        l_ref[...] = jnp.zeros(l_ref.shape, F32)
        acc_ref[...] = jnp.zeros(acc_ref.shape, F32)

    q = q_ref[0]
    same_head = (lax.broadcasted_iota(jnp.int32, (SAMPLE_LINES, KEY_LINES), 0) % N_HEADS
                 == lax.broadcasted_iota(jnp.int32, (SAMPLE_LINES, KEY_LINES), 1) % N_HEADS)
    spread = jnp.where(lax.broadcasted_iota(jnp.int32, (PAGE_SIZE, KEY_LINES), 1) // N_HEADS
                       == lax.broadcasted_iota(jnp.int32, (PAGE_SIZE, KEY_LINES), 0), 1.0, 0.0).astype(BF16)

    def page_logits(k_page, keep8):
        s = lax.dot_general(q, k_page, (((1,), (1,)), ((), ())), preferred_element_type=F32)
        k8 = jnp.dot(keep8, spread, preferred_element_type=F32)
        kl = jnp.concatenate([jnp.broadcast_to(k8[r:r + 1], (N_HEADS, KEY_LINES)) for r in range(SAMPLE_Q)], axis=0)
        return jnp.where(same_head & (kl > 0.5), s, NEG_BIG)

    def update(logits, v_pages):
        m_prev = m_ref[...]
        m_new = m_prev
        for s in logits:
            m_new = jnp.maximum(m_new, jnp.max(s, axis=1, keepdims=True))
        alpha = jnp.exp2(m_prev - m_new)
        l_new = alpha * l_ref[...]
        acc = alpha * acc_ref[...]
        for s, v_page in zip(logits, v_pages):
            p = jnp.exp2(s - m_new)
            l_new = l_new + jnp.sum(p, axis=1, keepdims=True)
            acc = acc + jnp.dot(p.astype(BF16), v_page, preferred_element_type=F32)
        m_ref[...] = m_new
        l_ref[...] = l_new
        acc_ref[...] = acc

    keep = keep_ref[0]
    def lines(page_ref):
        return page_ref[...].reshape(KEY_LINES, HEAD_DIM).astype(BF16)

    update([page_logits(lines(kp[r]), keep[:, r * PAGE_SIZE:(r + 1) * PAGE_SIZE]) for r in range(PAGES_PER_STEP)],
           [lines(vp[r]) for r in range(PAGES_PER_STEP)])

    @pl.when(j == pl.num_programs(1) - 1)
    def _():
        update([page_logits(knew_ref[0], keepn_ref[0])], [vnew_ref[0]])
        o_ref[0] = acc_ref[...] / l_ref[...]


def _sample_attn(pt_flat, q_lines, keep, knew_lines, vnew_lines, cache_k, cache_v, layer, n_pages):
    db = q_lines.shape[0]
    step_w = PAGES_PER_STEP * PAGE_SIZE
    grid_spec = pltpu.PrefetchScalarGridSpec(
        num_scalar_prefetch=1,
        grid=(db, n_pages // PAGES_PER_STEP),
        in_specs=[pl.BlockSpec((1, SAMPLE_LINES, HEAD_DIM), lambda b, j, pt: (b, 0, 0)),
                  pl.BlockSpec((1, SAMPLE_ROWS, step_w), lambda b, j, pt: (b, 0, j)),
                  pl.BlockSpec((1, SAMPLE_ROWS, PAGE_SIZE), lambda b, j, pt: (b, 0, n_pages)),
                  pl.BlockSpec((1, KEY_LINES, HEAD_DIM), lambda b, j, pt: (b, 0, 0)),
                  pl.BlockSpec((1, KEY_LINES, HEAD_DIM), lambda b, j, pt: (b, 0, 0))]
                 + 2 * _page_specs(layer, n_pages, (PAGE_SIZE, N_HEADS, HEAD_DIM)),
        out_specs=pl.BlockSpec((1, SAMPLE_LINES, HEAD_DIM), lambda b, j, pt: (b, 0, 0)),
        scratch_shapes=[pltpu.VMEM((SAMPLE_LINES, 1), F32), pltpu.VMEM((SAMPLE_LINES, 1), F32),
                        pltpu.VMEM((SAMPLE_LINES, HEAD_DIM), F32)],
    )
    return pl.pallas_call(
        _sample_attn_kernel,
        grid_spec=grid_spec,
        out_shape=jax.ShapeDtypeStruct((db, SAMPLE_LINES, HEAD_DIM), F32),
        compiler_params=_cparams(("arbitrary", "arbitrary"), 48),
        name="sample_attn",
    )(pt_flat, q_lines, keep, keep, knew_lines, vnew_lines, *([cache_k] * PAGES_PER_STEP),
      *([cache_v] * PAGES_PER_STEP))


def _cmul(ar, ai, br, bi):
    return ar * br - ai * bi, ar * bi + ai * br


def _shift_rows(x, s, row, fill):
    return jnp.where(row >= s, pltpu.roll(x, s, 0), fill)


def _s5_kernel(u_ref, h0r_ref, h0i_ref, lam_ref, wb_ref, wc_ref, d_ref, wg_ref, bg_ref,
               z_ref, sr_ref, si_ref, hs_ref, car_ref, *, last_row):
    c = pl.program_id(1)
    chunk = u_ref.shape[1]
    n = SSM_N

    @pl.when(c == 0)
    def _():
        car_ref[:, 0:n] = h0r_ref[0]
        car_ref[:, n:2 * n] = h0i_ref[0]

    lr = lam_ref[0:1, :]
    li = lam_ref[1:2, :]
    dt = jnp.exp(lam_ref[2:3, :])
    mag = jnp.exp(lr * dt)
    p1r = mag * jnp.cos(li * dt)
    p1i = mag * jnp.sin(li * dt)
    den = lr * lr + li * li
    fr = ((p1r - 1.0) * lr + p1i * li) / den
    fi = (p1i * lr - (p1r - 1.0) * li) / den
    p2r, p2i = _cmul(p1r, p1i, p1r, p1i)
    p4r, p4i = _cmul(p2r, p2i, p2r, p2i)
    p8r, p8i = _cmul(p4r, p4i, p4r, p4i)
    row = lax.broadcasted_iota(jnp.int32, (SUBLANES, 1), 0)
    pwr = jnp.ones((SUBLANES, n), F32)
    pwi = jnp.zeros((SUBLANES, n), F32)
    for bit, (qr, qi) in ((1, (p1r, p1i)), (2, (p2r, p2i)), (4, (p4r, p4i)), (8, (p8r, p8i))):
        nr, ni = _cmul(pwr, pwi, qr, qi)
        take = ((row + 1) & bit) != 0
        pwr = jnp.where(take, nr, pwr)
        pwi = jnp.where(take, ni, pwi)

    u = u_ref[0]
    hs_ref[...] = jnp.dot(u.astype(BF16), wb_ref[...], preferred_element_type=F32)

    def group(g, carry):
        cr, ci = carry
        r0 = pl.multiple_of(g * SUBLANES, SUBLANES)
        gr = hs_ref[pl.ds(r0, SUBLANES), 0:n]
        gi = hs_ref[pl.ds(r0, SUBLANES), n:2 * n]
        xr, xi = _cmul(fr, fi, gr, gi)
        for s, (qr, qi) in ((1, (p1r, p1i)), (2, (p2r, p2i)), (4, (p4r, p4i))):
            sr, si = _cmul(qr, qi, _shift_rows(xr, s, row, 0.0), _shift_rows(xi, s, row, 0.0))
            xr, xi = xr + sr, xi + si
        tr, ti = _cmul(pwr, pwi, cr, ci)
        hr, hi = xr + tr, xi + ti
        hs_ref[pl.ds(r0, SUBLANES), 0:n] = hr
        hs_ref[pl.ds(r0, SUBLANES), n:2 * n] = hi
        return hr[SUBLANES - 1:SUBLANES], hi[SUBLANES - 1:SUBLANES]

    cr, ci = lax.fori_loop(0, chunk // SUBLANES, group, (car_ref[:, 0:n], car_ref[:, n:2 * n]))
    car_ref[:, 0:n] = cr
    car_ref[:, n:2 * n] = ci

    y = jnp.dot(hs_ref[...].astype(BF16), wc_ref[...], preferred_element_type=F32) + d_ref[...] * u
    z = jax.nn.gelu(y)
    gate = jax.nn.sigmoid(jnp.dot(z.astype(BF16), wg_ref[...], preferred_element_type=F32) + bg_ref[...])
    z_ref[0] = z * gate

    @pl.when(c == pl.num_programs(1) - 1)
    def _():
        sr_ref[0] = hs_ref[last_row:last_row + 1, 0:n]
        si_ref[0] = hs_ref[last_row:last_row + 1, n:2 * n]


def _blockdiag(blocks):
    g, r, c = blocks.shape
    eye = jnp.eye(g, dtype=blocks.dtype)
    return (blocks[:, :, None, :] * eye[:, None, :, None]).reshape(g * r, g * c)


def _s5(u3, h0r, h0i, t_valid, lam_re, lam_im, log_dt, b_re, b_im, c_re, c_im, d_skip, w_glu, b_glu):
    b, t, _ = u3.shape
    chunk = min(256, t)
    last_row = (t_valid - 1) % chunk
    lam = jnp.concatenate([lam_re.reshape(1, SSM_N), lam_im.reshape(1, SSM_N),
                           jnp.broadcast_to(log_dt[:, None], (SSM_GROUPS, SSM_STATE)).reshape(1, SSM_N),
                           jnp.zeros((SUBLANES - 3, SSM_N), F32)], axis=0)
    wb = jnp.concatenate([_blockdiag(jnp.swapaxes(b_re, 1, 2)), _blockdiag(jnp.swapaxes(b_im, 1, 2))],
                         axis=1).astype(BF16)
    wc = jnp.concatenate([_blockdiag(jnp.swapaxes(c_re, 1, 2)), -_blockdiag(jnp.swapaxes(c_im, 1, 2))],
                         axis=0).astype(BF16)
    full = lambda a: pl.BlockSpec(a.shape, lambda bi, c: (0,) * a.ndim)
    seq = pl.BlockSpec((1, chunk, SSM_W), lambda bi, c: (bi, c, 0))
    st = pl.BlockSpec((1, 1, SSM_N), lambda bi, c: (bi, 0, 0))
    d2 = d_skip.reshape(1, SSM_W)
    wg = w_glu.astype(BF16)
    bg = b_glu.reshape(1, SSM_W)
    return pl.pallas_call(
        functools.partial(_s5_kernel, last_row=last_row),
        grid=(b, t // chunk),
        in_specs=[seq, st, st, full(lam), full(wb), full(wc), full(d2), full(wg), full(bg)],
        out_specs=(seq, st, st),
        out_shape=(jax.ShapeDtypeStruct((b, t, SSM_W), F32), jax.ShapeDtypeStruct((b, 1, SSM_N), F32),
                   jax.ShapeDtypeStruct((b, 1, SSM_N), F32)),
        scratch_shapes=[pltpu.VMEM((chunk, 2 * SSM_N), F32), pltpu.VMEM((1, 2 * SSM_N), F32)],
        compiler_params=_cparams(("arbitrary", "arbitrary"), 48),
        name="s5",
    )(u3, h0r.reshape(b, 1, SSM_N), h0i.reshape(b, 1, SSM_N), lam, wb, wc, d2, wg, bg)


def _lru_kernel(x_ref, xg_ref, buf_ref, h0_ref, cw_ref, vec_ref, wa_ref, wx_ref,
                y_ref, nb_ref, hl_ref, ext_ref, a_ref, b_ref, car_ref, *, last_row):
    c = pl.program_id(1)
    chunk = x_ref.shape[1]
    pad = SUBLANES

    @pl.when(c == 0)
    def _():
        ext_ref[0:pad, :] = jnp.zeros((pad, LRU_W), F32)
        ext_ref[pad - (CONV_W - 1):pad, :] = buf_ref[0]
        car_ref[...] = h0_ref[0]

    @pl.when(c > 0)
    def _():
        ext_ref[0:pad, :] = ext_ref[chunk:chunk + pad, :]

    ext_ref[pad:pad + chunk, :] = x_ref[0]
    xc = vec_ref[0:1, :] + cw_ref[CONV_W - 1:CONV_W, :] * x_ref[0]
    for jj in range(CONV_W - 1):
        xc = xc + cw_ref[jj:jj + 1, :] * ext_ref[pad - (CONV_W - 1) + jj:pad - (CONV_W - 1) + jj + chunk, :]
    xb = xc.astype(BF16)
    r = jax.nn.sigmoid(jnp.dot(xb, wa_ref[...], preferred_element_type=F32) + vec_ref[1:2, :])
    gi = jax.nn.sigmoid(jnp.dot(xb, wx_ref[...], preferred_element_type=F32) + vec_ref[2:3, :])
    lam = vec_ref[3:4, :]
    log_sig = -(jnp.maximum(-lam, 0.0) + jnp.log(1.0 + jnp.exp(-jnp.abs(lam))))
    log_a = LRU_C * r * log_sig
    a = jnp.exp(log_a)
    one_minus_a2 = -jnp.tanh(log_a) * (jnp.exp(2.0 * log_a) + 1.0)
    a_ref[...] = a
    b_ref[...] = jnp.sqrt(one_minus_a2) * (gi * xc)
    row = lax.broadcasted_iota(jnp.int32, (SUBLANES, 1), 0)

    def group(g, carry):
        r0 = pl.multiple_of(g * SUBLANES, SUBLANES)
        av = a_ref[pl.ds(r0, SUBLANES), :]
        bv = b_ref[pl.ds(r0, SUBLANES), :]
        for s in (1, 2, 4):
            bv = av * _shift_rows(bv, s, row, 0.0) + bv
            av = av * _shift_rows(av, s, row, 1.0)
        h = bv + av * carry
        b_ref[pl.ds(r0, SUBLANES), :] = h
        return h[SUBLANES - 1:SUBLANES]

    car_ref[...] = lax.fori_loop(0, chunk // SUBLANES, group, car_ref[...])
    y_ref[0] = b_ref[...] * jax.nn.gelu(xg_ref[0])

    @pl.when(c == pl.num_programs(1) - 1)
    def _():
        hl_ref[0] = b_ref[last_row:last_row + 1, :]
        lo = pad + last_row - (CONV_W - 2)
        nb_ref[0] = ext_ref[lo:lo + CONV_W - 1, :]


def _rglru(x3, xg3, conv_buf, h0, t_valid, conv_w, conv_b, w_a, b_a, w_x, b_x, lam):
    b, t, _ = x3.shape
    chunk = min(256, t)
    last_row = (t_valid - 1) % chunk
    vec = jnp.concatenate([conv_b[None], b_a[None], b_x[None], lam[None], jnp.zeros((SUBLANES - 4, LRU_W), F32)], 0)
    wa = _blockdiag(w_a).astype(BF16)
    wx = _blockdiag(w_x).astype(BF16)
    full = lambda a: pl.BlockSpec(a.shape, lambda bi, c: (0,) * a.ndim)
    seq = pl.BlockSpec((1, chunk, LRU_W), lambda bi, c: (bi, c, 0))
    return pl.pallas_call(
        functools.partial(_lru_kernel, last_row=last_row),
        grid=(b, t // chunk),
        in_specs=[seq, seq, pl.BlockSpec((1, CONV_W - 1, LRU_W), lambda bi, c: (bi, 0, 0)),
                  pl.BlockSpec((1, 1, LRU_W), lambda bi, c: (bi, 0, 0)), full(conv_w), full(vec), full(wa), full(wx)],
        out_specs=(seq, pl.BlockSpec((1, CONV_W - 1, LRU_W), lambda bi, c: (bi, 0, 0)),
                   pl.BlockSpec((1, 1, LRU_W), lambda bi, c: (bi, 0, 0))),
        out_shape=(jax.ShapeDtypeStruct((b, t, LRU_W), F32), jax.ShapeDtypeStruct((b, CONV_W - 1, LRU_W), F32),
                   jax.ShapeDtypeStruct((b, 1, LRU_W), F32)),
        scratch_shapes=[pltpu.VMEM((chunk + 2 * SUBLANES, LRU_W), F32), pltpu.VMEM((chunk, LRU_W), F32),
                        pltpu.VMEM((chunk, LRU_W), F32), pltpu.VMEM((1, LRU_W), F32)],
        compiler_params=_cparams(("arbitrary", "arbitrary"), 32),
        name="rglru",
    )(x3, xg3, conv_buf, h0.reshape(b, 1, LRU_W), conv_w, vec, wa, wx)


def _layer_norm(x, g, b):
    mu = jnp.mean(x, axis=-1, keepdims=True)
    xc = x - mu
    var = jnp.mean(xc * xc, axis=-1, keepdims=True)
    return xc * lax.rsqrt(var + LN_EPS) * g + b


def _mix_kernel(ya_ref, ys_ref, yl_ref, gl_ref, x_ref, wbr_ref, wout_ref, vec_ref, wr_ref, br_ref,
                x1_ref, x1b_ref, comb_ref, pos_ref, cnt_ref, *, alpha, tiles_per_block):
    i = pl.program_id(0)
    tm = x_ref.shape[0]
    mixed = None
    for n, y_ref in enumerate((ya_ref, ys_ref, yl_ref)):
        proj = jnp.dot(y_ref[...].astype(BF16), wbr_ref[n], preferred_element_type=F32)
        t = jax.nn.sigmoid(gl_ref[:, n * D_MODEL:(n + 1) * D_MODEL]) * proj
        mixed = t if mixed is None else mixed + t
    mixed = jnp.dot(mixed.astype(BF16), wout_ref[...], preferred_element_type=F32) + vec_ref[0:1, :]
    x1 = _layer_norm(alpha * x_ref[...] + mixed, vec_ref[1:2, :], vec_ref[2:3, :])
    x1_ref[...] = x1
    x1b_ref[...] = x1.astype(BF16)

    logits = jnp.dot(x1.astype(BF16), wr_ref[...].astype(BF16), preferred_element_type=F32) + br_ref[...]
    lane = lax.broadcasted_iota(jnp.int32, (tm, N_EXPERTS), 1)
    work = logits
    vals, hits = [], []
    for _ in range(TOP_K):
        v = jnp.max(work, axis=1, keepdims=True)
        ix = jnp.min(jnp.where(work == v, lane, N_EXPERTS), axis=1, keepdims=True)
        hit = lane == ix
        vals.append(v)
        hits.append(hit)
        work = jnp.where(hit, -jnp.inf, work)
    es = [jnp.exp(v - vals[0]) for v in vals]
    den = es[0] + es[1] + es[2] + es[3]
    comb = jnp.zeros((tm, N_EXPERTS), F32)
    routed = jnp.zeros((tm, N_EXPERTS), F32)
    for e, hit in zip(es, hits):
        comb = comb + jnp.where(hit, e / den, 0.0)
        routed = routed + jnp.where(hit, 1.0, 0.0)
    comb_ref[...] = comb

    @pl.when(i % tiles_per_block == 0)
    def _():
        cnt_ref[...] = jnp.zeros(cnt_ref.shape, F32)

    before = (lax.broadcasted_iota(jnp.int32, (tm, tm), 1) < lax.broadcasted_iota(jnp.int32, (tm, tm), 0))
    rank = jnp.dot(jnp.where(before, 1.0, 0.0).astype(BF16), routed.astype(BF16), preferred_element_type=F32)
    pos = rank + cnt_ref[...]
    pos_ref[...] = jnp.where(routed > 0.5, pos, -1.0)
    cnt_ref[...] = cnt_ref[...] + jnp.sum(routed, axis=0, keepdims=True)


def _mix(ya, ys, yl, gl, x, w_branch, w_out, b_out, ln_g, ln_b, w_r, b_r, alpha, moe_block):
    n = x.shape[0]
    tm = min(256, n)
    row = lambda wd: pl.BlockSpec((tm, wd), lambda i: (i, 0))
    full = lambda a: pl.BlockSpec(a.shape, lambda i: (0,) * a.ndim)
    vec = jnp.concatenate([b_out[None], ln_g[None], ln_b[None], jnp.zeros((SUBLANES - 3, D_MODEL), F32)], 0)
    wbr = w_branch.astype(BF16)
    wout = w_out.astype(BF16)
    br = b_r.reshape(1, N_EXPERTS)
    return pl.pallas_call(
        functools.partial(_mix_kernel, alpha=alpha, tiles_per_block=moe_block // tm),
        grid=(n // tm,),
        in_specs=[row(ATT_W), row(SSM_W), row(LRU_W), row(N_BRANCH * D_MODEL), row(D_MODEL),
                  full(wbr), full(wout), full(vec), full(w_r), full(br)],
        out_specs=(row(D_MODEL), row(D_MODEL), row(N_EXPERTS), row(N_EXPERTS)),
        out_shape=(jax.ShapeDtypeStruct((n, D_MODEL), F32), jax.ShapeDtypeStruct((n, D_MODEL), BF16),
                   jax.ShapeDtypeStruct((n, N_EXPERTS), F32), jax.ShapeDtypeStruct((n, N_EXPERTS), F32)),
        scratch_shapes=[pltpu.VMEM((1, N_EXPERTS), F32)],
        compiler_params=_cparams(("arbitrary",), 48),
        name="mix_ln1_router",
    )(ya, ys, yl, gl, x, wbr, wout, vec, w_r, br)


MOE_BLOCK = 2048
MOE_CHUNK = 256
MOE_TAIL = 64
MOE_SCATTER = 512


def _moe_kernel(cnt_ref, xb_ref, post_ref, pos_ref, comb_ref, wgu_ref, bgu_ref, wdn_ref, bdn_ref, o_ref,
                *, ch, ch_tail, sb):
    tb_i = pl.program_id(0)
    e = pl.program_id(1)
    tb = xb_ref.shape[0]

    @pl.when(e == 0)
    def _():
        o_ref[...] = jnp.zeros(o_ref.shape, F32)

    cnt = cnt_ref[tb_i * N_EXPERTS + e]
    full = cnt // ch
    rem = cnt % ch
    short = jnp.logical_and(rem > 0, rem <= ch_tail)
    widen = jnp.logical_and(short, full > 0)
    tail_only = jnp.logical_and(short, full == 0)
    n_main = full + jnp.where(jnp.logical_and(rem > 0, jnp.logical_not(short)), 1, 0) - jnp.where(widen, 1, 0)
    lane_e = lax.broadcasted_iota(jnp.int32, (1, N_EXPERTS), 1) == e
    slot_row = post_ref[0]

    def run_chunk(base, ch):
        pick = (lax.broadcasted_iota(jnp.int32, (ch, tb), 0) + base).astype(F32) == slot_row
        xc = jnp.dot(jnp.where(pick, 1.0, 0.0).astype(BF16), xb_ref[...], preferred_element_type=F32).astype(BF16)
        gu = jnp.dot(xc, wgu_ref[0], preferred_element_type=F32) + bgu_ref[0]
        gate = jnp.minimum(gu[:, :D_FF], SWIGLU_LIMIT)
        up = jnp.clip(gu[:, D_FF:], -SWIGLU_LIMIT, SWIGLU_LIMIT)
        act = (up + 1.0) * gate * jax.nn.sigmoid(SWIGLU_ALPHA * gate)
        y = (jnp.dot(act.astype(BF16), wdn_ref[0], preferred_element_type=F32) + bdn_ref[0]).astype(BF16)
        for s in range(tb // sb):
            rows = slice(s * sb, (s + 1) * sb)
            slot_col = jnp.sum(jnp.where(lane_e, pos_ref[rows, :], 0.0), axis=1, keepdims=True)
            w_col = jnp.sum(jnp.where(lane_e, comb_ref[rows, :], 0.0), axis=1, keepdims=True)
            upd = None
            for k0 in range(0, ch, MOE_CHUNK):
                kw = min(MOE_CHUNK, ch - k0)
                put = (lax.broadcasted_iota(jnp.int32, (sb, kw), 1) + (base + k0)).astype(F32) == slot_col
                t = jnp.dot(jnp.where(put, w_col, 0.0).astype(BF16), y[k0:k0 + kw], preferred_element_type=F32)
                upd = t if upd is None else upd + t
            o_ref[rows, :] += upd

    def main_body(c, carry):
        run_chunk(c * ch, ch)
        return carry

    lax.fori_loop(0, n_main, main_body, 0)

    @pl.when(widen)
    def _():
        run_chunk(n_main * ch, ch + ch_tail)

    @pl.when(tail_only)
    def _():
        run_chunk(0, ch_tail)


def _moe(x1b, comb, pos, wgu, bgu, wdn, bdn):
    n = x1b.shape[0]
    tb = min(MOE_BLOCK, n)
    ch = min(MOE_CHUNK, tb)
    ch_tail = min(MOE_TAIL, ch)
    sb = min(MOE_SCATTER, tb)
    n_tb = n // tb
    routed = (pos >= 0).astype(jnp.int32)
    cnt = routed.reshape(n_tb, tb, N_EXPERTS).sum(axis=1).reshape(-1)
    pos_t = pos.T.reshape(N_EXPERTS, 1, n)
    grid_spec = pltpu.PrefetchScalarGridSpec(
        num_scalar_prefetch=1,
        grid=(n_tb, N_EXPERTS),
        in_specs=[pl.BlockSpec((tb, D_MODEL), lambda t, e, c: (t, 0)),
                  pl.BlockSpec((1, 1, tb), lambda t, e, c: (e, 0, t)),
                  pl.BlockSpec((tb, N_EXPERTS), lambda t, e, c: (t, 0)),
                  pl.BlockSpec((tb, N_EXPERTS), lambda t, e, c: (t, 0)),
                  pl.BlockSpec((1, D_MODEL, 2 * D_FF), lambda t, e, c: (e, 0, 0)),
                  pl.BlockSpec((1, 1, 2 * D_FF), lambda t, e, c: (e, 0, 0)),
                  pl.BlockSpec((1, D_FF, D_MODEL), lambda t, e, c: (e, 0, 0)),
                  pl.BlockSpec((1, 1, D_MODEL), lambda t, e, c: (e, 0, 0))],
        out_specs=pl.BlockSpec((tb, D_MODEL), lambda t, e, c: (t, 0)),
    )
    return pl.pallas_call(
        functools.partial(_moe_kernel, ch=ch, ch_tail=ch_tail, sb=sb),
        grid_spec=grid_spec,
        out_shape=jax.ShapeDtypeStruct((n, D_MODEL), F32),
        compiler_params=_cparams(("arbitrary", "arbitrary"), 56),
        name="moe",
    )(cnt, x1b, pos_t, pos, comb, wgu, bgu, wdn, bdn)


def _ln2_kernel(x_ref, m_ref, vec_ref, o_ref, *, alpha):
    o_ref[...] = _layer_norm(alpha * x_ref[...] + m_ref[...], vec_ref[0:1, :], vec_ref[1:2, :])


def _ln2(x1, moe_out, g, b, alpha):
    n = x1.shape[0]
    tm = min(512, n)
    vec = jnp.concatenate([g[None], b[None], jnp.zeros((SUBLANES - 2, D_MODEL), F32)], 0)
    row = pl.BlockSpec((tm, D_MODEL), lambda i: (i, 0))
    return pl.pallas_call(
        functools.partial(_ln2_kernel, alpha=alpha),
        grid=(n // tm,),
        in_specs=[row, row, pl.BlockSpec(vec.shape, lambda i: (0, 0))],
        out_specs=row,
        out_shape=jax.ShapeDtypeStruct((n, D_MODEL), F32),
        compiler_params=_cparams(("arbitrary",), 32),
        name="ln2",
    )(x1, moe_out, vec)


def _pad_rows(a, rows):
    pad = [(0, 0)] * a.ndim
    pad[1] = (0, rows - a.shape[1])
    return jnp.pad(a, pad)


def kernel(x_prompt, x_sample, cache_k, cache_v, cache_kidx, state_ssm_re, state_ssm_im, state_lru_h, state_lru_conv, page_table, w_in, b_in, ssm_lam_re, ssm_lam_im, ssm_log_dt, ssm_b_re, ssm_b_im, ssm_c_re, ssm_c_im, ssm_d, ssm_w_glu, ssm_b_glu, lru_conv_w, lru_conv_b, lru_w_a, lru_b_a, lru_w_x, lru_b_x, lru_lam, w_branch, w_out, b_out, ln1_g, ln1_b, moe_w_r, moe_b_r, moe_w_gu, moe_b_gu, moe_w_dn, moe_b_dn, ln2_g, ln2_b):
    depth = w_in.shape[0]
    alpha = (2 * depth) ** 0.25
    bp, tp, _ = x_prompt.shape
    db, ns, _ = x_sample.shape
    assert ns <= SAMPLE_Q
    n_pages = page_table.shape[1]
    past = n_pages * PAGE_SIZE
    n_pool = cache_k.shape[1]
    pt_flat = page_table.reshape(-1).astype(jnp.int32)
    topk_p = min(TOPK_MAX, tp // 4)
    topk_s = min(TOPK_MAX, (past + ns) // 4)

    q_end = 3 * ATT_W
    i_end = q_end + IDX_KI + IDX_DIM + IDX_HEADS

    def regroup(a):
        padw = [(0, 0)] * (a.ndim - 1) + [(0, IDX_PAD - (i_end - q_end))]
        return jnp.concatenate([a[..., :q_end], jnp.pad(a[..., q_end:i_end], padw), a[..., i_end:]], axis=-1)

    w_in_r = regroup(w_in).astype(BF16)
    b_in_r = regroup(b_in).reshape(depth, 1, N_IN_PAD)
    wgu_b = moe_w_gu.astype(BF16)
    wdn_b = moe_w_dn.astype(BF16)
    bgu3 = moe_b_gu.reshape(depth, N_EXPERTS, 1, 2 * D_FF)
    bdn3 = moe_b_dn.reshape(depth, N_EXPERTS, 1, D_MODEL)

    xp = x_prompt.reshape(bp * tp, D_MODEL)
    xs = x_sample.reshape(db * ns, D_MODEL)
    new_p = [[] for _ in range(7)]
    new_s = [[] for _ in range(7)]

    def channel_mix(l, x, ya, ys, yl, gl):
        n = x.shape[0]
        x1, x1b, comb, pos = _mix(ya, ys, yl, gl, x, w_branch[l], w_out[l], b_out[l], ln1_g[l], ln1_b[l],
                                  moe_w_r[l], moe_b_r[l], alpha, min(MOE_BLOCK, n))
        moe_out = _moe(x1b, comb, pos, wgu_b[l], bgu3[l], wdn_b[l], bdn3[l])
        return _ln2(x1, moe_out, ln2_g[l], ln2_b[l], alpha)

    def key_lines(a):
        a = _pad_rows(a.reshape(db, ns, N_HEADS, HEAD_DIM), PAGE_SIZE)
        return a.reshape(db, KEY_LINES, HEAD_DIM).astype(BF16)

    for l in range(depth):
        ssm_p = (ssm_lam_re[l], ssm_lam_im[l], ssm_log_dt[l], ssm_b_re[l], ssm_b_im[l], ssm_c_re[l],
                 ssm_c_im[l], ssm_d[l], ssm_w_glu[l], ssm_b_glu[l])
        lru_p = (lru_conv_w[l], lru_conv_b[l], lru_w_a[l], lru_b_a[l], lru_w_x[l], lru_b_x[l], lru_lam[l])

        q_hm, k, k_hm, v, vt_hm, idx, u, xr, xg, gl = _in_proj(xp, w_in_r[l], b_in_r[l])
        r3 = lambda a: a.reshape(bp, tp, a.shape[-1])
        idx3 = r3(idx)
        wi_t = _pad_rows(jnp.swapaxes(idx3[:, :, IDX_WI:IDX_WI + IDX_HEADS], 1, 2), SUBLANES)
        bias_t = _dsa_index_t(idx3, wi_t, topk_p)
        ya = jnp.swapaxes(_dsa_attn_t(q_hm, k_hm, vt_hm, bias_t), 1, 2).reshape(bp * tp, ATT_W)
        zeros_n = jnp.zeros((bp, SSM_N), F32)
        ys, s_re, s_im = _s5(r3(u), zeros_n, zeros_n, tp, *ssm_p)
        yl, nbuf, hl = _rglru(r3(xr), r3(xg), jnp.zeros((bp, CONV_W - 1, LRU_W), F32), jnp.zeros((bp, LRU_W), F32),
                              tp, *lru_p)
        st = (k.reshape(bp, tp, N_HEADS, HEAD_DIM), v.reshape(bp, tp, N_HEADS, HEAD_DIM),
              idx3[:, :, IDX_KI:IDX_KI + IDX_DIM], s_re.reshape(bp, SSM_GROUPS, SSM_STATE),
              s_im.reshape(bp, SSM_GROUPS, SSM_STATE), nbuf, hl.reshape(bp, LRU_W))
        for jj in range(7):
            new_p[jj].append(st[jj])
        xp = channel_mix(l, xp, ya, ys.reshape(bp * tp, SSM_W), yl.reshape(bp * tp, LRU_W), gl)

        q_hm, k, _, v, _, idx, u, xr, xg, gl = _in_proj(xs, w_in_r[l], b_in_r[l])
        r3 = lambda a: a.reshape(db, ns, a.shape[-1])
        idx3 = r3(idx)
        ki_new = idx3[:, :, IDX_KI:IDX_KI + IDX_DIM]
        keep = _sample_index(pt_flat, _pad_rows(idx3, SAMPLE_ROWS), _pad_rows(ki_new, PAGE_SIZE), cache_kidx, l,
                             n_pages, topk_s, ns)
        q_lines = _pad_rows(jnp.transpose(q_hm.reshape(N_HEADS, db, ns, HEAD_DIM), (1, 2, 0, 3)), SAMPLE_Q)
        ya = _sample_attn(pt_flat, q_lines.reshape(db, SAMPLE_LINES, HEAD_DIM), keep, key_lines(k), key_lines(v),
                          cache_k, cache_v, l, n_pages)
        ya = ya.reshape(db, SAMPLE_Q, ATT_W)[:, :ns].reshape(db * ns, ATT_W)
        ys, s_re, s_im = _s5(_pad_rows(r3(u), SAMPLE_ROWS), state_ssm_re[l].reshape(db, SSM_N),
                             state_ssm_im[l].reshape(db, SSM_N), ns, *ssm_p)
        yl, nbuf, hl = _rglru(_pad_rows(r3(xr), SAMPLE_ROWS), _pad_rows(r3(xg), SAMPLE_ROWS), state_lru_conv[l],
                              state_lru_h[l], ns, *lru_p)
        st = (k.reshape(db, ns, N_HEADS, HEAD_DIM), v.reshape(db, ns, N_HEADS, HEAD_DIM), ki_new,
              s_re.reshape(db, SSM_GROUPS, SSM_STATE), s_im.reshape(db, SSM_GROUPS, SSM_STATE), nbuf,
              hl.reshape(db, LRU_W))
        for jj in range(7):
            new_s[jj].append(st[jj])
        xs = channel_mix(l, xs, ya, ys[:, :ns].reshape(db * ns, SSM_W), yl[:, :ns].reshape(db * ns, LRU_W), gl)

    k_p, v_p, ki_p, sre_p, sim_p, lc_p, lh_p = [jnp.stack(a) for a in new_p]
    k_s, v_s, ki_s, sre_s, sim_s, lc_s, lh_s = [jnp.stack(a) for a in new_s]
    return (xp.reshape(bp, tp, D_MODEL), xs.reshape(db, ns, D_MODEL), k_p, v_p, ki_p, sre_p, sim_p, lh_p, lc_p,
            k_s, v_s, ki_s, sre_s, sim_s, lh_s, lc_s)
```

```python
import functools
import math

import jax
import jax.numpy as jnp
from jax import lax
from jax.experimental import pallas as pl
from jax.experimental.pallas import tpu as pltpu

F32 = jnp.float32
BF16 = jnp.bfloat16

D_MODEL = 1024
N_HEADS = 8
HEAD_DIM = 64
ATT_W = N_HEADS * HEAD_DIM
IDX_HEADS = 4
IDX_DIM = 64
TOPK_MAX = 256
PAGE_SIZE = 128
SSM_W = 512
SSM_GROUPS = 32
SSM_GROUP = 16
SSM_STATE = 64
SSM_N = SSM_GROUPS * SSM_STATE
LRU_W = 512
CONV_W = 4
LRU_C = 8.0
N_EXPERTS = 32
TOP_K = 4
D_FF = 1024
SWIGLU_LIMIT = 7.0
SWIGLU_ALPHA = 1.702
N_BRANCH = 3
LN_EPS = 1e-5
IDX_PAD = 384
IDX_KI = IDX_HEADS * IDX_DIM
IDX_WI = IDX_KI + IDX_DIM
NEG_BIG = -1e30
M_INIT = -1e29
QK_SCALE = HEAD_DIM ** -0.5 * math.log2(math.e)
SUBLANES = 8
LANES = 128


def _cparams(sem, vmem_mb):
    return pltpu.CompilerParams(dimension_semantics=sem, vmem_limit_bytes=vmem_mb * 1024 * 1024)


_SEG_W = (ATT_W, ATT_W, ATT_W, IDX_PAD, SSM_W, LRU_W, LRU_W, N_BRANCH * D_MODEL)
_SEG_OFF = tuple(sum(_SEG_W[:i]) for i in range(len(_SEG_W)))
N_IN_PAD = sum(_SEG_W)


def _in_proj_kernel(x_ref, w_ref, b_ref, wvt_ref, bvt_ref, q_ref, k_ref, kb_ref, v_ref, vt_ref, idx_ref, u_ref,
                    xr_ref, xg_ref, gl_ref):
    xb = x_ref[...].astype(BF16)

    def seg(i):
        off, wd = _SEG_OFF[i], _SEG_W[i]
        return jnp.dot(xb, w_ref[:, off:off + wd], preferred_element_type=F32) + b_ref[:, off:off + wd]

    tm = x_ref.shape[0]
    q = seg(0) * QK_SCALE
    k = seg(1)
    v = seg(2)
    k_ref[...] = k
    v_ref[...] = v
    vt = lax.dot_general(wvt_ref[...], xb, (((1,), (1,)), ((), ())), preferred_element_type=F32) + bvt_ref[...]
    ones = jnp.ones((HEAD_DIM, tm), F32)
    for h in range(N_HEADS):
        sl = slice(h * HEAD_DIM, (h + 1) * HEAD_DIM)
        q_ref[h] = q[:, sl].astype(BF16)
        kb_ref[h] = k[:, sl].astype(BF16)
        vt_ref[h] = jnp.concatenate([vt[sl, :], ones], axis=0).astype(BF16)
    idx_ref[...] = seg(3)
    u_ref[...] = seg(4)
    xr_ref[...] = seg(5)
    xg_ref[...] = seg(6)
    gl_ref[...] = seg(7)


def _in_proj(x, w, b):
    n = x.shape[0]
    tm = min(256, n)
    row = lambda wd: pl.BlockSpec((tm, wd), lambda i: (i, 0))
    full = lambda a: pl.BlockSpec(a.shape, lambda i: (0,) * a.ndim)
    heads = lambda wd: pl.BlockSpec((N_HEADS, tm, wd), lambda i: (0, i, 0))
    v_off = _SEG_OFF[2]
    wvt = w[:, v_off:v_off + ATT_W].T
    bvt = b[:, v_off:v_off + ATT_W].T
    out_shape = (
        jax.ShapeDtypeStruct((N_HEADS, n, HEAD_DIM), BF16),
        jax.ShapeDtypeStruct((n, ATT_W), F32),
        jax.ShapeDtypeStruct((N_HEADS, n, HEAD_DIM), BF16),
        jax.ShapeDtypeStruct((n, ATT_W), F32),
        jax.ShapeDtypeStruct((N_HEADS, 2 * HEAD_DIM, n), BF16),
        jax.ShapeDtypeStruct((n, IDX_PAD), F32),
        jax.ShapeDtypeStruct((n, SSM_W), F32),
        jax.ShapeDtypeStruct((n, LRU_W), F32),
        jax.ShapeDtypeStruct((n, LRU_W), F32),
        jax.ShapeDtypeStruct((n, N_BRANCH * D_MODEL), F32),
    )
    return pl.pallas_call(
        _in_proj_kernel,
        grid=(n // tm,),
        in_specs=[row(D_MODEL), full(w), full(b), full(wvt), full(bvt)],
        out_specs=tuple(pl.BlockSpec((N_HEADS, 2 * HEAD_DIM, tm), lambda i: (0, 0, i)) if s.shape[2:] == (n,)
                        else heads(s.shape[2]) if len(s.shape) == 3 else row(s.shape[1]) for s in out_shape),
        out_shape=out_shape,
        compiler_params=_cparams(("arbitrary",), 56),
        name="in_proj",
    )(x, w, b, wvt, bvt)


BISECT_STEPS = 20


def _prefix_matrix(ck):
    r = lax.broadcasted_iota(jnp.int32, (ck, ck), 0)
    c = lax.broadcasted_iota(jnp.int32, (ck, ck), 1)
    return jnp.where(r <= c, 1.0, 0.0).astype(BF16)


def _select_rows(s_ref, rows, rb, n_chunks, ck, kk, n_adm):
    nl = ck // LANES

    def tile(c, j):
        return s_ref[rows, pl.ds(pl.multiple_of(c * ck + j * LANES, LANES), LANES)]

    def lanes(x):
        return jnp.broadcast_to(x, (rb, LANES))

    def count_gt(x):
        xb = lanes(x)

        def body(c, acc):
            for j in range(nl):
                acc = acc + lax.shift_right_arithmetic(pltpu.bitcast(xb - tile(c, j), jnp.int32), 31)
            return acc
        neg = lax.fori_loop(0, n_chunks, body, jnp.zeros((rb, LANES), jnp.int32))
        return -jnp.sum(neg.astype(F32), axis=1, keepdims=True)

    def min_above_max_upto(lo, hi):
        lob, hib = lanes(lo), lanes(hi)

        def body(c, acc):
            mn, mx = acc
            for j in range(nl):
                v = tile(c, j)
                mn = jnp.minimum(mn, jnp.where(v > lob, v, jnp.inf))
                mx = jnp.maximum(mx, jnp.where(v <= hib, v, -jnp.inf))
            return mn, mx
        mn, mx = lax.fori_loop(0, n_chunks, body, (jnp.full((rb, LANES), jnp.inf, F32),
                                                   jnp.full((rb, LANES), -jnp.inf, F32)))
        return jnp.min(mn, axis=1, keepdims=True), jnp.max(mx, axis=1, keepdims=True)

    def bisect_at(x, st):
        lo, hi, clo, chi = st
        c = count_gt(x)
        ge = c >= kk
        return (jnp.where(ge, x, lo), jnp.where(ge, hi, x), jnp.where(ge, c, clo), jnp.where(ge, chi, c))

    inf = jnp.full((rb, 1), jnp.inf, F32)
    row_min, row_max = min_above_max_upto(-inf, inf)
    st = (row_min - 1.0 - jnp.abs(row_min), row_max, n_adm, jnp.zeros((rb, 1), F32))
    st = lax.fori_loop(0, BISECT_STEPS, lambda _, s: bisect_at(0.5 * s[0] + 0.5 * s[1], s), st)

    def refine_cond(carry):
        return carry[3] > 0.5

    def refine(carry):
        st, a, b, _ = carry
        x = 0.5 * a + 0.5 * b
        st = bisect_at(jnp.where(x >= b, a, x), st)
        a, b = min_above_max_upto(st[0], st[1])
        return st, a, b, jnp.max(jnp.where(a == b, 0.0, 1.0))

    a, b = min_above_max_upto(st[0], st[1])
    st, tau, _, _ = lax.while_loop(refine_cond, refine, (st, a, b, jnp.max(jnp.where(a == b, 0.0, 1.0))))

    return tau, kk - st[3]


def _emit_selection(s_ref, n_chunks, ck, tau, need, emit):
    rows = s_ref.shape[0]
    tri = _prefix_matrix(ck)

    def body(c, seen):
        v = s_ref[:, pl.ds(pl.multiple_of(c * ck, LANES), ck)]
        eq = v == tau
        pref = jnp.dot(jnp.where(eq, 1.0, 0.0).astype(BF16), tri, preferred_element_type=F32) + seen
        emit(c, (v > tau) | (eq & (pref <= need)))
        return pref[:, ck - 1:ck]

    lax.fori_loop(0, n_chunks, body, jnp.zeros((rows, 1), F32))


def _indexer_scores(qi, wi, keys):
    sc = None
    for h in range(IDX_HEADS):
        d = lax.dot_general(qi[:, h * IDX_DIM:(h + 1) * IDX_DIM], keys, (((1,), (1,)), ((), ())),
                            preferred_element_type=F32)
        t = wi[:, h:h + 1] * jnp.maximum(d, 0.0)
        sc = t if sc is None else sc + t
    return sc


DSA_TQ = 256
DSA_CK = 512
DSA_RB = 64


def _dsa_index_kernel(q_ref, kw_ref, bias_ref, s_ref, tau_ref, need_ref, *, topk):
    tq, t_len = s_ref.shape
    ck = min(DSA_CK, t_len)
    rb = min(DSA_RB, tq)
    i = pl.program_id(1)
    t0 = i * tq
    n_valid = ((i + 1) * tq + ck - 1) // ck
    qi = q_ref[0, :, 0:IDX_KI].astype(BF16)
    wi = q_ref[0, :, IDX_WI:IDX_WI + IDX_HEADS]
    t_ids = t0 + lax.broadcasted_iota(jnp.int32, (tq, 1), 0)
    lane = lax.broadcasted_iota(jnp.int32, (1, ck), 1)

    def score_body(c, carry):
        off = pl.multiple_of(c * ck, LANES)
        keys = kw_ref[0, pl.ds(off, ck), 0:IDX_DIM].astype(BF16)
        sc = _indexer_scores(qi, wi, keys)
        s_ref[:, pl.ds(off, ck)] = jnp.where(lane + c * ck <= t_ids, sc, -jnp.inf)
        return carry

    lax.fori_loop(0, n_valid, score_body, 0)

    pair = (t_len // ck) % 2 == 0
    if pair:
        @pl.when(n_valid % 2 == 1)
        def _():
            s_ref[:, pl.ds(pl.multiple_of(n_valid * ck, LANES), ck)] = jnp.full((tq, ck), -jnp.inf, F32)
    n_search, ck_search = ((n_valid + 1) // 2, 2 * ck) if pair else (n_valid, ck)

    def block_body(bi, carry):
        r0 = pl.multiple_of(bi * rb, rb)
        rows = pl.ds(r0, rb)
        n_adm = (t0 + r0 + 1 + lax.broadcasted_iota(jnp.int32, (rb, 1), 0)).astype(F32)
        tau, need = _select_rows(s_ref, rows, rb, n_search, ck_search, jnp.minimum(n_adm, float(topk)), n_adm)
        tau_ref[rows, :] = tau
        need_ref[rows, :] = need
        return carry

    lax.fori_loop(0, tq // rb, block_body, 0)

    def emit(c, keep):
        bias_ref[0, :, pl.ds(pl.multiple_of(c * ck, LANES), ck)] = jnp.where(keep, 0.0, NEG_BIG).astype(BF16)

    _emit_selection(s_ref, n_valid, ck, tau_ref[...], need_ref[...], emit)

    def fill_body(c, carry):
        bias_ref[0, :, pl.ds(pl.multiple_of(c * ck, LANES), ck)] = jnp.full((tq, ck), NEG_BIG, BF16)
        return carry

    lax.fori_loop(n_valid, t_len // ck, fill_body, 0)


def _dsa_index(idx3, topk):
    b, t, _ = idx3.shape
    tq = min(DSA_TQ, t)
    return pl.pallas_call(
        functools.partial(_dsa_index_kernel, topk=topk),
        grid=(b, t // tq),
        in_specs=[pl.BlockSpec((1, tq, IDX_PAD), lambda bi, i: (bi, i, 0)),
                  pl.BlockSpec((1, t, LANES), lambda bi, i: (bi, 0, IDX_KI // LANES))],
        out_specs=pl.BlockSpec((1, tq, t), lambda bi, i: (bi, i, 0)),
        out_shape=jax.ShapeDtypeStruct((b, t, t), BF16),
        scratch_shapes=[pltpu.VMEM((tq, t), F32), pltpu.VMEM((tq, 1), F32), pltpu.VMEM((tq, 1), F32)],
        compiler_params=_cparams(("arbitrary", "arbitrary"), 48),
        name="dsa_index",
    )(idx3, idx3)


def _dsa_attn_kernel(q_ref, k_ref, v_ref, bias_ref, o_ref, m_ref, acc_ref, bias_scr):
    tq = q_ref.shape[1]
    tk = k_ref.shape[1]
    i = pl.program_id(1)
    j = pl.program_id(2)
    last = ((i + 1) * tq - 1) // tk

    @pl.when(j == 0)
    def _():
        m_ref[...] = jnp.full(m_ref.shape, M_INIT, F32)
        acc_ref[...] = jnp.zeros(acc_ref.shape, F32)

    @pl.when(j <= last)
    def _():
        bias_scr[...] = bias_ref[0].astype(F32)
        for h in range(N_HEADS):
            s = lax.dot_general(q_ref[h], k_ref[h], (((1,), (1,)), ((), ())), preferred_element_type=F32)
            s = s + bias_scr[...]
            m_prev = m_ref[h]
            m_new = jnp.maximum(m_prev, jnp.max(s, axis=1, keepdims=True))
            p = jnp.exp2(s - m_new)
            acc_ref[h] = jnp.exp2(m_prev - m_new) * acc_ref[h] + jnp.dot(p.astype(BF16), v_ref[h],
                                                                         preferred_element_type=F32)
            m_ref[h] = m_new

    @pl.when(j == pl.num_programs(2) - 1)
    def _():
        for h in range(N_HEADS):
            acc = acc_ref[h]
            o_ref[0, :, h * HEAD_DIM:(h + 1) * HEAD_DIM] = acc[:, 0:HEAD_DIM] / acc[:, HEAD_DIM:HEAD_DIM + 1]


def _dsa_attn(q_hm, k_hm, v_hm, bias):
    b, t, _ = bias.shape
    tq = min(DSA_TQ, t)
    tk = min(DSA_CK, t)
    nq, nk = t // tq, t // tk
    last = lambda i: ((i + 1) * tq - 1) // tk
    kv_idx = lambda bi, i, j: (0, bi * nk + jnp.minimum(j, last(i)), 0)
    return pl.pallas_call(
        _dsa_attn_kernel,
        grid=(b, nq, nk),
        in_specs=[pl.BlockSpec((N_HEADS, tq, HEAD_DIM), lambda bi, i, j: (0, bi * nq + i, 0)),
                  pl.BlockSpec((N_HEADS, tk, HEAD_DIM), kv_idx),
                  pl.BlockSpec((N_HEADS, tk, 2 * HEAD_DIM), kv_idx),
                  pl.BlockSpec((1, tq, tk), lambda bi, i, j: (bi, i, jnp.minimum(j, last(i))))],
        out_specs=pl.BlockSpec((1, tq, ATT_W), lambda bi, i, j: (bi, i, 0)),
        out_shape=jax.ShapeDtypeStruct((b, t, ATT_W), F32),
        scratch_shapes=[pltpu.VMEM((N_HEADS, tq, 1), F32), pltpu.VMEM((N_HEADS, tq, 2 * HEAD_DIM), F32),
                        pltpu.VMEM((tq, tk), F32)],
        compiler_params=_cparams(("arbitrary", "arbitrary", "arbitrary"), 32),
        name="dsa_attn",
    )(q_hm, k_hm, v_hm, bias)


N_ACC = 4
ATT_TQ = 512
ATT_TK = 1024


def _select_cols(s_ref, cols, n_chunks, ck, kk, n_adm):
    groups = ck // SUBLANES

    def bc(x):
        return jnp.broadcast_to(x, (SUBLANES, LANES))

    def reduce_keys(fn, op, init, dtype):
        def body(c, accs):
            accs = list(accs)
            slab = s_ref.at[pl.ds(pl.multiple_of(c * ck, ck), ck), cols]
            for g in range(groups):
                accs[g % N_ACC] = op(accs[g % N_ACC], fn(slab[g * SUBLANES:(g + 1) * SUBLANES, :]))
            return tuple(accs)
        accs = lax.fori_loop(0, n_chunks, body, tuple(jnp.full((SUBLANES, LANES), init, dtype) for _ in range(N_ACC)))
        out = accs[0]
        for a in accs[1:]:
            out = op(out, a)
        return out

    def count_gt(x):
        xb = bc(x)
        neg = reduce_keys(lambda v: lax.shift_right_arithmetic(pltpu.bitcast(xb - v, jnp.int32), 31), jnp.add, 0,
                          jnp.int32)
        return -jnp.sum(neg.astype(F32), axis=0, keepdims=True)

    def min_above(lo):
        lob = bc(lo)
        return jnp.min(reduce_keys(lambda v: jnp.where(v > lob, v, jnp.inf), jnp.minimum, jnp.inf, F32),
                       axis=0, keepdims=True)

    def max_upto(hi):
        hib = bc(hi)
        return jnp.max(reduce_keys(lambda v: jnp.where(v <= hib, v, -jnp.inf), jnp.maximum, -jnp.inf, F32),
                       axis=0, keepdims=True)

    def bisect_at(x, st):
        lo, hi, clo, chi = st
        c = count_gt(x)
        ge = c >= kk
        return (jnp.where(ge, x, lo), jnp.where(ge, hi, x), jnp.where(ge, c, clo), jnp.where(ge, chi, c))

    inf = jnp.full((1, LANES), jnp.inf, F32)
    col_min, col_max = min_above(-inf), max_upto(inf)
    st = (col_min - 1.0 - jnp.abs(col_min), col_max, n_adm, jnp.zeros((1, LANES), F32))
    st = lax.fori_loop(0, BISECT_STEPS, lambda _, s: bisect_at(0.5 * s[0] + 0.5 * s[1], s), st)

    def refine(carry):
        st, a, b, _ = carry
        x = 0.5 * a + 0.5 * b
        st = bisect_at(jnp.where(x >= b, a, x), st)
        a, b = min_above(st[0]), max_upto(st[1])
        return st, a, b, jnp.max(jnp.where(a == b, 0.0, 1.0))

    a, b = min_above(st[0]), max_upto(st[1])
    st, tau, _, _ = lax.while_loop(lambda carry: carry[3] > 0.5, refine,
                                   (st, a, b, jnp.max(jnp.where(a == b, 0.0, 1.0))))
    return tau, kk - st[3]


def _dsa_index_kernel_t(q_ref, wi_ref, kw_ref, bias_ref, s_ref, tau_ref, need_ref, *, topk):
    t_len, tq = s_ref.shape
    ck = min(DSA_CK, t_len)
    i = pl.program_id(1)
    t0 = i * tq
    n_valid = ((i + 1) * tq + ck - 1) // ck
    qi = q_ref[0, :, 0:IDX_KI].astype(BF16)
    wi = wi_ref[0]
    t_ids = t0 + lax.broadcasted_iota(jnp.int32, (1, tq), 1)
    key_row = lax.broadcasted_iota(jnp.int32, (ck, 1), 0)

    def score_body(c, carry):
        off = pl.multiple_of(c * ck, ck)
        keys = kw_ref[0, pl.ds(off, ck), 0:IDX_DIM].astype(BF16)
        sc = None
        for h in range(IDX_HEADS):
            d = lax.dot_general(keys, qi[:, h * IDX_DIM:(h + 1) * IDX_DIM], (((1,), (1,)), ((), ())),
                                preferred_element_type=F32)
            t = wi[h:h + 1, :] * jnp.maximum(d, 0.0)
            sc = t if sc is None else sc + t
        s_ref[pl.ds(off, ck), :] = jnp.where(key_row + c * ck <= t_ids, sc, -jnp.inf)
        return carry

    lax.fori_loop(0, n_valid, score_body, 0)

    for lb in range(tq // LANES):
        cols = slice(lb * LANES, (lb + 1) * LANES)
        n_adm = (t0 + lb * LANES + 1 + lax.broadcasted_iota(jnp.int32, (1, LANES), 1)).astype(F32)
        tau, need = _select_cols(s_ref, cols, n_valid, ck, jnp.minimum(n_adm, float(topk)), n_adm)
        tau_ref[:, cols] = tau
        need_ref[:, cols] = need

    tau = tau_ref[...]
    need = need_ref[...]
    before = (lax.broadcasted_iota(jnp.int32, (ck, ck), 1) <= lax.broadcasted_iota(jnp.int32, (ck, ck), 0))
    tri = jnp.where(before, 1.0, 0.0).astype(BF16)

    def emit_body(c, seen):
        off = pl.multiple_of(c * ck, ck)
        v = s_ref[pl.ds(off, ck), :]
        eq = v == tau
        pref = jnp.dot(tri, jnp.where(eq, 1.0, 0.0).astype(BF16), preferred_element_type=F32) + seen
        keep = (v > tau) | (eq & (pref <= need))
        bias_ref[0, pl.ds(off, ck), :] = jnp.where(keep, 0.0, NEG_BIG).astype(BF16)
        return pref[ck - 1:ck, :]

    lax.fori_loop(0, n_valid, emit_body, jnp.zeros((1, tq), F32))

    def fill_body(c, carry):
        bias_ref[0, pl.ds(pl.multiple_of(c * ck, ck), ck), :] = jnp.full((ck, tq), NEG_BIG, BF16)
        return carry

    lax.fori_loop(n_valid, t_len // ck, fill_body, 0)


def _dsa_index_t(idx3, wi_t, topk):
    b, t, _ = idx3.shape
    tq = min(DSA_TQ, t)
    return pl.pallas_call(
        functools.partial(_dsa_index_kernel_t, topk=topk),
        grid=(b, t // tq),
        in_specs=[pl.BlockSpec((1, tq, IDX_PAD), lambda bi, i: (bi, i, 0)),
                  pl.BlockSpec((1, SUBLANES, tq), lambda bi, i: (bi, 0, i)),
                  pl.BlockSpec((1, t, LANES), lambda bi, i: (bi, 0, IDX_KI // LANES))],
        out_specs=pl.BlockSpec((1, t, tq), lambda bi, i: (bi, 0, i)),
        out_shape=jax.ShapeDtypeStruct((b, t, t), BF16),
        scratch_shapes=[pltpu.VMEM((t, tq), F32), pltpu.VMEM((1, tq), F32), pltpu.VMEM((1, tq), F32)],
        compiler_params=_cparams(("arbitrary", "arbitrary"), 48),
        name="dsa_index",
    )(idx3, wi_t, idx3)


def _dsa_attn_kernel_t(q_ref, k_ref, vt_ref, bias_ref, o_ref, bias_scr, *state):
    m_refs, acc_refs = state[:N_HEADS], state[N_HEADS:]
    tq = q_ref.shape[1]
    tk = k_ref.shape[1]
    i = pl.program_id(1)
    j = pl.program_id(2)
    last = ((i + 1) * tq - 1) // tk

    @pl.when(j == 0)
    def _():
        for h in range(N_HEADS):
            m_refs[h][...] = jnp.full(m_refs[h].shape, M_INIT, F32)
            acc_refs[h][...] = jnp.zeros(acc_refs[h].shape, F32)

    @pl.when(j <= last)
    def _():
        bias = bias_ref[0].astype(F32)

        def qk(h):
            return lax.dot_general(k_ref[h], q_ref[h], (((1,), (1,)), ((), ())), preferred_element_type=F32)

        s_next = qk(0)
        for h in range(N_HEADS):
            s = s_next + bias
            if h + 1 < N_HEADS:
                s_next = qk(h + 1)
            m_prev = m_refs[h][...]
            m_new = jnp.maximum(m_prev, jnp.max(s, axis=0, keepdims=True))
            p = jnp.exp2(s - m_new)
            acc_refs[h][...] = (jnp.exp2(m_prev - m_new) * acc_refs[h][...]
                                + jnp.dot(vt_ref[h], p.astype(BF16), preferred_element_type=F32))
            m_refs[h][...] = m_new

    @pl.when(j == pl.num_programs(2) - 1)
    def _():
        for h in range(N_HEADS):
            acc = acc_refs[h][...]
            o_ref[0, h * HEAD_DIM:(h + 1) * HEAD_DIM, :] = acc[0:HEAD_DIM, :] / acc[HEAD_DIM:HEAD_DIM + 1, :]


def _dsa_attn_t(q_hm, k_hm, vt_hm, bias_t):
    b, t, _ = bias_t.shape
    tq = min(ATT_TQ, t)
    tk = min(ATT_TK, t)
    nq, nk = t // tq, t // tk
    last = lambda i: ((i + 1) * tq - 1) // tk
    return pl.pallas_call(
        _dsa_attn_kernel_t,
        grid=(b, nq, nk),
        in_specs=[pl.BlockSpec((N_HEADS, tq, HEAD_DIM), lambda bi, i, j: (0, bi * nq + i, 0)),
                  pl.BlockSpec((N_HEADS, tk, HEAD_DIM), lambda bi, i, j: (0, bi * nk + jnp.minimum(j, last(i)), 0)),
                  pl.BlockSpec((N_HEADS, 2 * HEAD_DIM, tk), lambda bi, i, j: (0, 0, bi * nk + jnp.minimum(j, last(i)))),
                  pl.BlockSpec((1, tk, tq), lambda bi, i, j: (bi, jnp.minimum(j, last(i)), i))],
        out_specs=pl.BlockSpec((1, ATT_W, tq), lambda bi, i, j: (bi, 0, i)),
        out_shape=jax.ShapeDtypeStruct((b, ATT_W, t), F32),
        scratch_shapes=[pltpu.VMEM((tk, tq), F32)] + [pltpu.VMEM((1, tq), F32)] * N_HEADS
                       + [pltpu.VMEM((2 * HEAD_DIM, tq), F32)] * N_HEADS,
        compiler_params=_cparams(("arbitrary", "arbitrary", "arbitrary"), 32),
        name="dsa_attn",
    )(q_hm, k_hm, vt_hm, bias_t)


PAGES_PER_STEP = 8
SAMPLE_ROWS = 8
SAMPLE_CK = 640


def _sample_index_kernel(pt_ref, q_ref, knew_ref, *refs, topk, n_new):
    pages = refs[:PAGES_PER_STEP]
    keep_ref, s_ref = refs[PAGES_PER_STEP:]
    j = pl.program_id(1)
    width = s_ref.shape[1]
    past = width - PAGE_SIZE
    step_w = PAGES_PER_STEP * PAGE_SIZE
    q = q_ref[0]
    wi = q[:, IDX_WI:IDX_WI + IDX_HEADS]
    q4 = jnp.concatenate([q[:, h * IDX_DIM:(h + 1) * IDX_DIM] for h in range(IDX_HEADS)], axis=0).astype(BF16)

    def weighted(d):
        sc = None
        for h in range(IDX_HEADS):
            t = wi[:, h:h + 1] * jnp.maximum(d[h * SAMPLE_ROWS:(h + 1) * SAMPLE_ROWS], 0.0)
            sc = t if sc is None else sc + t
        return sc

    s_ref[:, pl.ds(pl.multiple_of(j * step_w, LANES), step_w)] = jnp.concatenate(
        [weighted(jnp.dot(q4, p[...].astype(BF16), preferred_element_type=F32)) for p in pages], axis=1)

    @pl.when(j == pl.num_programs(1) - 1)
    def _():
        sc = weighted(lax.dot_general(q4, knew_ref[0].astype(BF16), (((1,), (1,)), ((), ())),
                                      preferred_element_type=F32))
        row = lax.broadcasted_iota(jnp.int32, (SAMPLE_ROWS, PAGE_SIZE), 0)
        lane = lax.broadcasted_iota(jnp.int32, (SAMPLE_ROWS, PAGE_SIZE), 1)
        ok = (lane <= row) & (lane < n_new)
        s_ref[:, past:past + PAGE_SIZE] = jnp.where(ok, sc, -jnp.inf)
        r1 = lax.broadcasted_iota(jnp.int32, (SAMPLE_ROWS, 1), 0)
        n_adm = (past + jnp.minimum(r1 + 1, n_new)).astype(F32)

        def emit(c, keep):
            keep_ref[0, :, pl.ds(pl.multiple_of(c * SAMPLE_CK, LANES), SAMPLE_CK)] = jnp.where(keep, 1.0, 0.0).astype(BF16)

        tau, need = _select_rows(s_ref, pl.ds(0, SAMPLE_ROWS), SAMPLE_ROWS, width // SAMPLE_CK, SAMPLE_CK,
                                 jnp.full((SAMPLE_ROWS, 1), float(topk), F32), n_adm)
        _emit_selection(s_ref, width // SAMPLE_CK, SAMPLE_CK, tau, need, emit)


def _page_specs(layer, n_pages, block):
    def spec(r):
        def index_map(b, j, pt):
            return (layer, pt[b * n_pages + j * PAGES_PER_STEP + r]) + (0,) * len(block)
        return pl.BlockSpec((None, None) + tuple(block), index_map)

    return [spec(r) for r in range(PAGES_PER_STEP)]


def _sample_index(pt_flat, idx_pad, knew_pad, cache_kidx, layer, n_pages, topk, n_new):
    db = idx_pad.shape[0]
    width = n_pages * PAGE_SIZE + PAGE_SIZE
    assert width % SAMPLE_CK == 0 and n_pages % PAGES_PER_STEP == 0
    grid_spec = pltpu.PrefetchScalarGridSpec(
        num_scalar_prefetch=1,
        grid=(db, n_pages // PAGES_PER_STEP),
        in_specs=[pl.BlockSpec((1, SAMPLE_ROWS, IDX_PAD), lambda b, j, pt: (b, 0, 0)),
                  pl.BlockSpec((1, PAGE_SIZE, IDX_DIM), lambda b, j, pt: (b, 0, 0))]
                 + _page_specs(layer, n_pages, (IDX_DIM, PAGE_SIZE)),
        out_specs=pl.BlockSpec((1, SAMPLE_ROWS, width), lambda b, j, pt: (b, 0, 0)),
        scratch_shapes=[pltpu.VMEM((SAMPLE_ROWS, width), F32)],
    )
    return pl.pallas_call(
        functools.partial(_sample_index_kernel, topk=topk, n_new=n_new),
        grid_spec=grid_spec,
        out_shape=jax.ShapeDtypeStruct((db, SAMPLE_ROWS, width), BF16),
        compiler_params=_cparams(("arbitrary", "arbitrary"), 32),
        name="sample_index",
    )(pt_flat, idx_pad, knew_pad, *([cache_kidx] * PAGES_PER_STEP))


SAMPLE_Q = 4
SAMPLE_LINES = SAMPLE_Q * N_HEADS
KEY_LINES = PAGE_SIZE * N_HEADS


def _sample_attn_kernel(pt_ref, q_ref, keep_ref, keepn_ref, knew_ref, vnew_ref, *refs):
    kp = refs[:PAGES_PER_STEP]
    vp = refs[PAGES_PER_STEP:2 * PAGES_PER_STEP]
    o_ref, m_ref, l_ref, acc_ref = refs[2 * PAGES_PER_STEP:]
    j = pl.program_id(1)

    @pl.when(j == 0)
    def _():
        m_ref[...] = jnp.full(m_ref.shape, M_INIT, F32)
        l_ref[...] = jnp.zeros(l_ref.shape, F32)
        acc_ref[...] = jnp.zeros(acc_ref.shape, F32)

    q = q_ref[0]
    same_head = (lax.broadcasted_iota(jnp.int32, (SAMPLE_LINES, KEY_LINES), 0) % N_HEADS
                 == lax.broadcasted_iota(jnp.int32, (SAMPLE_LINES, KEY_LINES), 1) % N_HEADS)
    spread = jnp.where(lax.broadcasted_iota(jnp.int32, (PAGE_SIZE, KEY_LINES), 1) // N_HEADS
                       == lax.broadcasted_iota(jnp.int32, (PAGE_SIZE, KEY_LINES), 0), 1.0, 0.0).astype(BF16)

    def page_logits(k_page, keep8):
        s = lax.dot_general(q, k_page, (((1,), (1,)), ((), ())), preferred_element_type=F32)
        k8 = jnp.dot(keep8, spread, preferred_element_type=F32)
        kl = jnp.concatenate([jnp.broadcast_to(k8[r:r + 1], (N_HEADS, KEY_LINES)) for r in range(SAMPLE_Q)], axis=0)
        return jnp.where(same_head & (kl > 0.5), s, NEG_BIG)

    def update(logits, v_pages):
        m_prev = m_ref[...]
        m_new = m_prev
        for s in logits:
            m_new = jnp.maximum(m_new, jnp.max(s, axis=1, keepdims=True))
        alpha = jnp.exp2(m_prev - m_new)
        l_new = alpha * l_ref[...]
        acc = alpha * acc_ref[...]
        for s, v_page in zip(logits, v_pages):
            p = jnp.exp2(s - m_new)
            l_new = l_new + jnp.sum(p, axis=1, keepdims=True)
            acc = acc + jnp.dot(p.astype(BF16), v_page, preferred_element_type=F32)
        m_ref[...] = m_new
        l_ref[...] = l_new
        acc_ref[...] = acc

    keep = keep_ref[0]
    def lines(page_ref):
        return page_ref[...].reshape(KEY_LINES, HEAD_DIM).astype(BF16)

    update([page_logits(lines(kp[r]), keep[:, r * PAGE_SIZE:(r + 1) * PAGE_SIZE]) for r in range(PAGES_PER_STEP)],
           [lines(vp[r]) for r in range(PAGES_PER_STEP)])

    @pl.when(j == pl.num_programs(1) - 1)
    def _():
        update([page_logits(knew_ref[0], keepn_ref[0])], [vnew_ref[0]])
        o_ref[0] = acc_ref[...] / l_ref[...]


def _sample_attn(pt_flat, q_lines, keep, knew_lines, vnew_lines, cache_k, cache_v, layer, n_pages):
    db = q_lines.shape[0]
    step_w = PAGES_PER_STEP * PAGE_SIZE
    grid_spec = pltpu.PrefetchScalarGridSpec(
        num_scalar_prefetch=1,
        grid=(db, n_pages // PAGES_PER_STEP),
        in_specs=[pl.BlockSpec((1, SAMPLE_LINES, HEAD_DIM), lambda b, j, pt: (b, 0, 0)),
                  pl.BlockSpec((1, SAMPLE_ROWS, step_w), lambda b, j, pt: (b, 0, j)),
                  pl.BlockSpec((1, SAMPLE_ROWS, PAGE_SIZE), lambda b, j, pt: (b, 0, n_pages)),
                  pl.BlockSpec((1, KEY_LINES, HEAD_DIM), lambda b, j, pt: (b, 0, 0)),
                  pl.BlockSpec((1, KEY_LINES, HEAD_DIM), lambda b, j, pt: (b, 0, 0))]
                 + 2 * _page_specs(layer, n_pages, (PAGE_SIZE, N_HEADS, HEAD_DIM)),
        out_specs=pl.BlockSpec((1, SAMPLE_LINES, HEAD_DIM), lambda b, j, pt: (b, 0, 0)),
        scratch_shapes=[pltpu.VMEM((SAMPLE_LINES, 1), F32), pltpu.VMEM((SAMPLE_LINES, 1), F32),
                        pltpu.VMEM((SAMPLE_LINES, HEAD_DIM), F32)],
    )
    return pl.pallas_call(
        _sample_attn_kernel,
        grid_spec=grid_spec,
        out_shape=jax.ShapeDtypeStruct((db, SAMPLE_LINES, HEAD_DIM), F32),
        compiler_params=_cparams(("arbitrary", "arbitrary"), 48),
        name="sample_attn",
    )(pt_flat, q_lines, keep, keep, knew_lines, vnew_lines, *([cache_k] * PAGES_PER_STEP),
      *([cache_v] * PAGES_PER_STEP))


def _sample_attn_kernel_t(pt_ref, q_ref, keep_ref, keepn_ref, knew_ref, vnew_ref, *refs):
    kp = refs[:PAGES_PER_STEP]
    vp = refs[PAGES_PER_STEP:2 * PAGES_PER_STEP]
    o_ref, m_ref, l_ref, acc_ref = refs[2 * PAGES_PER_STEP:]
    j = pl.program_id(1)
    own = (lax.broadcasted_iota(jnp.int32, (N_HEADS, ATT_W), 1) // HEAD_DIM
           == lax.broadcasted_iota(jnp.int32, (N_HEADS, ATT_W), 0))

    @pl.when(j == 0)
    def _():
        m_ref[...] = jnp.full(m_ref.shape, M_INIT, F32)
        l_ref[...] = jnp.zeros(l_ref.shape, F32)
        acc_ref[...] = jnp.zeros(acc_ref.shape, F32)

    q = q_ref[0].astype(F32)
    qbd = jnp.concatenate([jnp.where(own, jnp.broadcast_to(q[r:r + 1], (N_HEADS, ATT_W)), 0.0)
                           for r in range(SAMPLE_Q)], axis=0).astype(BF16)

    def update(s, keep8, pv):
        kf = keep8.astype(F32)
        kl = jnp.concatenate([jnp.broadcast_to(kf[r:r + 1], (N_HEADS, kf.shape[1])) for r in range(SAMPLE_Q)], axis=0)
        s = jnp.where(kl > 0.5, s, NEG_BIG)
        m_prev = m_ref[...]
        m_new = jnp.maximum(m_prev, jnp.max(s, axis=1, keepdims=True))
        alpha = jnp.exp2(m_prev - m_new)
        p = jnp.exp2(s - m_new)
        l_ref[...] = alpha * l_ref[...] + jnp.sum(p, axis=1, keepdims=True)
        acc_ref[...] = alpha * acc_ref[...] + pv(p.astype(BF16))
        m_ref[...] = m_new

    def page(ref):
        return ref[...].reshape(ATT_W, PAGE_SIZE).astype(BF16)

    s = jnp.concatenate([jnp.dot(qbd, page(kp[r]), preferred_element_type=F32) for r in range(PAGES_PER_STEP)], axis=1)

    def pv_pages(p):
        out = None
        for r in range(PAGES_PER_STEP):
            t = lax.dot_general(p[:, r * PAGE_SIZE:(r + 1) * PAGE_SIZE], page(vp[r]), (((1,), (1,)), ((), ())),
                                preferred_element_type=F32)
            out = t if out is None else out + t
        return out

    update(s, keep_ref[0], pv_pages)

    @pl.when(j == pl.num_programs(1) - 1)
    def _():
        s_new = lax.dot_general(qbd, knew_ref[0], (((1,), (1,)), ((), ())), preferred_element_type=F32)
        update(s_new, keepn_ref[0], lambda p: jnp.dot(p, vnew_ref[0], preferred_element_type=F32))
        o = acc_ref[...] / l_ref[...]
        outs = [jnp.sum(jnp.where(own, o[r * N_HEADS:(r + 1) * N_HEADS], 0.0), axis=0, keepdims=True)
                for r in range(SAMPLE_Q)]
        o_ref[0] = jnp.concatenate(outs + [jnp.zeros((SAMPLE_ROWS - SAMPLE_Q, ATT_W), F32)], axis=0)


def _sample_attn_t(pt_flat, q_pad, keep, knew_pad, vnew_pad, cache_kt, cache_vt, layer, n_pages):
    db = q_pad.shape[0]
    step_w = PAGES_PER_STEP * PAGE_SIZE
    grid_spec = pltpu.PrefetchScalarGridSpec(
        num_scalar_prefetch=1,
        grid=(db, n_pages // PAGES_PER_STEP),
        in_specs=[pl.BlockSpec((1, SAMPLE_ROWS, ATT_W), lambda b, j, pt: (b, 0, 0)),
                  pl.BlockSpec((1, SAMPLE_ROWS, step_w), lambda b, j, pt: (b, 0, j)),
                  pl.BlockSpec((1, SAMPLE_ROWS, PAGE_SIZE), lambda b, j, pt: (b, 0, n_pages)),
                  pl.BlockSpec((1, PAGE_SIZE, ATT_W), lambda b, j, pt: (b, 0, 0)),
                  pl.BlockSpec((1, PAGE_SIZE, ATT_W), lambda b, j, pt: (b, 0, 0))]
                 + 2 * _page_specs(layer, n_pages, (N_HEADS, HEAD_DIM, PAGE_SIZE)),
        out_specs=pl.BlockSpec((1, SAMPLE_ROWS, ATT_W), lambda b, j, pt: (b, 0, 0)),
        scratch_shapes=[pltpu.VMEM((SAMPLE_LINES, 1), F32), pltpu.VMEM((SAMPLE_LINES, 1), F32),
                        pltpu.VMEM((SAMPLE_LINES, ATT_W), F32)],
    )
    return pl.pallas_call(
        _sample_attn_kernel_t,
        grid_spec=grid_spec,
        out_shape=jax.ShapeDtypeStruct((db, SAMPLE_ROWS, ATT_W), F32),
        compiler_params=_cparams(("arbitrary", "arbitrary"), 32),
        name="sample_attn",
    )(pt_flat, q_pad, keep, keep, knew_pad, vnew_pad, *([cache_kt] * PAGES_PER_STEP), *([cache_vt] * PAGES_PER_STEP))


def _cmul(ar, ai, br, bi):
    return ar * br - ai * bi, ar * bi + ai * br


def _shift_rows(x, s, row, fill):
    return jnp.where(row >= s, pltpu.roll(x, s, 0), fill)


def _s5_kernel(u_ref, h0r_ref, h0i_ref, lam_ref, wb_ref, wc_ref, d_ref, wg_ref, bg_ref,
               z_ref, sr_ref, si_ref, hs_ref, car_ref, *, last_row):
    c = pl.program_id(1)
    chunk = u_ref.shape[1]
    n = SSM_N

    @pl.when(c == 0)
    def _():
        car_ref[:, 0:n] = h0r_ref[0]
        car_ref[:, n:2 * n] = h0i_ref[0]

    lr = lam_ref[0:1, :]
    li = lam_ref[1:2, :]
    dt = jnp.exp(lam_ref[2:3, :])
    mag = jnp.exp(lr * dt)
    p1r = mag * jnp.cos(li * dt)
    p1i = mag * jnp.sin(li * dt)
    den = lr * lr + li * li
    fr = ((p1r - 1.0) * lr + p1i * li) / den
    fi = (p1i * lr - (p1r - 1.0) * li) / den
    p2r, p2i = _cmul(p1r, p1i, p1r, p1i)
    p4r, p4i = _cmul(p2r, p2i, p2r, p2i)
    p8r, p8i = _cmul(p4r, p4i, p4r, p4i)
    row = lax.broadcasted_iota(jnp.int32, (SUBLANES, 1), 0)
    pwr = jnp.ones((SUBLANES, n), F32)
    pwi = jnp.zeros((SUBLANES, n), F32)
    for bit, (qr, qi) in ((1, (p1r, p1i)), (2, (p2r, p2i)), (4, (p4r, p4i)), (8, (p8r, p8i))):
        nr, ni = _cmul(pwr, pwi, qr, qi)
        take = ((row + 1) & bit) != 0
        pwr = jnp.where(take, nr, pwr)
        pwi = jnp.where(take, ni, pwi)

    u = u_ref[0]
    hs_ref[...] = jnp.dot(u.astype(BF16), wb_ref[...], preferred_element_type=F32)

    def group(g, carry):
        cr, ci = carry
        r0 = pl.multiple_of(g * SUBLANES, SUBLANES)
        gr = hs_ref[pl.ds(r0, SUBLANES), 0:n]
        gi = hs_ref[pl.ds(r0, SUBLANES), n:2 * n]
        xr, xi = _cmul(fr, fi, gr, gi)
        for s, (qr, qi) in ((1, (p1r, p1i)), (2, (p2r, p2i)), (4, (p4r, p4i))):
            sr, si = _cmul(qr, qi, _shift_rows(xr, s, row, 0.0), _shift_rows(xi, s, row, 0.0))
            xr, xi = xr + sr, xi + si
        tr, ti = _cmul(pwr, pwi, cr, ci)
        hr, hi = xr + tr, xi + ti
        hs_ref[pl.ds(r0, SUBLANES), 0:n] = hr
        hs_ref[pl.ds(r0, SUBLANES), n:2 * n] = hi
        return hr[SUBLANES - 1:SUBLANES], hi[SUBLANES - 1:SUBLANES]

    cr, ci = lax.fori_loop(0, chunk // SUBLANES, group, (car_ref[:, 0:n], car_ref[:, n:2 * n]))
    car_ref[:, 0:n] = cr
    car_ref[:, n:2 * n] = ci

    y = jnp.dot(hs_ref[...].astype(BF16), wc_ref[...], preferred_element_type=F32) + d_ref[...] * u
    z = jax.nn.gelu(y)
    gate = jax.nn.sigmoid(jnp.dot(z.astype(BF16), wg_ref[...], preferred_element_type=F32) + bg_ref[...])
    z_ref[0] = z * gate

    @pl.when(c == pl.num_programs(1) - 1)
    def _():
        sr_ref[0] = hs_ref[last_row:last_row + 1, 0:n]
        si_ref[0] = hs_ref[last_row:last_row + 1, n:2 * n]


def _blockdiag(blocks):
    g, r, c = blocks.shape
    eye = jnp.eye(g, dtype=blocks.dtype)
    return (blocks[:, :, None, :] * eye[:, None, :, None]).reshape(g * r, g * c)


def _s5(u3, h0r, h0i, t_valid, lam_re, lam_im, log_dt, b_re, b_im, c_re, c_im, d_skip, w_glu, b_glu):
    b, t, _ = u3.shape
    chunk = min(256, t)
    last_row = (t_valid - 1) % chunk
    lam = jnp.concatenate([lam_re.reshape(1, SSM_N), lam_im.reshape(1, SSM_N),
                           jnp.broadcast_to(log_dt[:, None], (SSM_GROUPS, SSM_STATE)).reshape(1, SSM_N),
                           jnp.zeros((SUBLANES - 3, SSM_N), F32)], axis=0)
    wb = jnp.concatenate([_blockdiag(jnp.swapaxes(b_re, 1, 2)), _blockdiag(jnp.swapaxes(b_im, 1, 2))],
                         axis=1).astype(BF16)
    wc = jnp.concatenate([_blockdiag(jnp.swapaxes(c_re, 1, 2)), -_blockdiag(jnp.swapaxes(c_im, 1, 2))],
                         axis=0).astype(BF16)
    full = lambda a: pl.BlockSpec(a.shape, lambda bi, c: (0,) * a.ndim)
    seq = pl.BlockSpec((1, chunk, SSM_W), lambda bi, c: (bi, c, 0))
    st = pl.BlockSpec((1, 1, SSM_N), lambda bi, c: (bi, 0, 0))
    d2 = d_skip.reshape(1, SSM_W)
    wg = w_glu.astype(BF16)
    bg = b_glu.reshape(1, SSM_W)
    return pl.pallas_call(
        functools.partial(_s5_kernel, last_row=last_row),
        grid=(b, t // chunk),
        in_specs=[seq, st, st, full(lam), full(wb), full(wc), full(d2), full(wg), full(bg)],
        out_specs=(seq, st, st),
        out_shape=(jax.ShapeDtypeStruct((b, t, SSM_W), F32), jax.ShapeDtypeStruct((b, 1, SSM_N), F32),
                   jax.ShapeDtypeStruct((b, 1, SSM_N), F32)),
        scratch_shapes=[pltpu.VMEM((chunk, 2 * SSM_N), F32), pltpu.VMEM((1, 2 * SSM_N), F32)],
        compiler_params=_cparams(("arbitrary", "arbitrary"), 48),
        name="s5",
    )(u3, h0r.reshape(b, 1, SSM_N), h0i.reshape(b, 1, SSM_N), lam, wb, wc, d2, wg, bg)


def _lru_kernel(x_ref, xg_ref, buf_ref, h0_ref, cw_ref, vec_ref, wa_ref, wx_ref,
                y_ref, nb_ref, hl_ref, ext_ref, a_ref, b_ref, car_ref, *, last_row):
    c = pl.program_id(1)
    chunk = x_ref.shape[1]
    pad = SUBLANES

    @pl.when(c == 0)
    def _():
        ext_ref[0:pad, :] = jnp.zeros((pad, LRU_W), F32)
        ext_ref[pad - (CONV_W - 1):pad, :] = buf_ref[0]
        car_ref[...] = h0_ref[0]

    @pl.when(c > 0)
    def _():
        ext_ref[0:pad, :] = ext_ref[chunk:chunk + pad, :]

    ext_ref[pad:pad + chunk, :] = x_ref[0]
    xc = vec_ref[0:1, :] + cw_ref[CONV_W - 1:CONV_W, :] * x_ref[0]
    for jj in range(CONV_W - 1):
        xc = xc + cw_ref[jj:jj + 1, :] * ext_ref[pad - (CONV_W - 1) + jj:pad - (CONV_W - 1) + jj + chunk, :]
    xb = xc.astype(BF16)
    r = jax.nn.sigmoid(jnp.dot(xb, wa_ref[...], preferred_element_type=F32) + vec_ref[1:2, :])
    gi = jax.nn.sigmoid(jnp.dot(xb, wx_ref[...], preferred_element_type=F32) + vec_ref[2:3, :])
    lam = vec_ref[3:4, :]
    log_sig = -(jnp.maximum(-lam, 0.0) + jnp.log(1.0 + jnp.exp(-jnp.abs(lam))))
    log_a = LRU_C * r * log_sig
    a = jnp.exp(log_a)
    one_minus_a2 = -jnp.tanh(log_a) * (jnp.exp(2.0 * log_a) + 1.0)
    a_ref[...] = a
    b_ref[...] = jnp.sqrt(one_minus_a2) * (gi * xc)
    row = lax.broadcasted_iota(jnp.int32, (SUBLANES, 1), 0)

    def group(g, carry):
        r0 = pl.multiple_of(g * SUBLANES, SUBLANES)
        av = a_ref[pl.ds(r0, SUBLANES), :]
        bv = b_ref[pl.ds(r0, SUBLANES), :]
        for s in (1, 2, 4):
            bv = av * _shift_rows(bv, s, row, 0.0) + bv
            av = av * _shift_rows(av, s, row, 1.0)
        h = bv + av * carry
        b_ref[pl.ds(r0, SUBLANES), :] = h
        return h[SUBLANES - 1:SUBLANES]

    car_ref[...] = lax.fori_loop(0, chunk // SUBLANES, group, car_ref[...])
    y_ref[0] = b_ref[...] * jax.nn.gelu(xg_ref[0])

    @pl.when(c == pl.num_programs(1) - 1)
    def _():
        hl_ref[0] = b_ref[last_row:last_row + 1, :]
        lo = pad + last_row - (CONV_W - 2)
        nb_ref[0] = ext_ref[lo:lo + CONV_W - 1, :]


def _rglru(x3, xg3, conv_buf, h0, t_valid, conv_w, conv_b, w_a, b_a, w_x, b_x, lam):
    b, t, _ = x3.shape
    chunk = min(256, t)
    last_row = (t_valid - 1) % chunk
    vec = jnp.concatenate([conv_b[None], b_a[None], b_x[None], lam[None], jnp.zeros((SUBLANES - 4, LRU_W), F32)], 0)
    wa = _blockdiag(w_a).astype(BF16)
    wx = _blockdiag(w_x).astype(BF16)
    full = lambda a: pl.BlockSpec(a.shape, lambda bi, c: (0,) * a.ndim)
    seq = pl.BlockSpec((1, chunk, LRU_W), lambda bi, c: (bi, c, 0))
    return pl.pallas_call(
        functools.partial(_lru_kernel, last_row=last_row),
        grid=(b, t // chunk),
        in_specs=[seq, seq, pl.BlockSpec((1, CONV_W - 1, LRU_W), lambda bi, c: (bi, 0, 0)),
                  pl.BlockSpec((1, 1, LRU_W), lambda bi, c: (bi, 0, 0)), full(conv_w), full(vec), full(wa), full(wx)],
        out_specs=(seq, pl.BlockSpec((1, CONV_W - 1, LRU_W), lambda bi, c: (bi, 0, 0)),
                   pl.BlockSpec((1, 1, LRU_W), lambda bi, c: (bi, 0, 0))),
        out_shape=(jax.ShapeDtypeStruct((b, t, LRU_W), F32), jax.ShapeDtypeStruct((b, CONV_W - 1, LRU_W), F32),
                   jax.ShapeDtypeStruct((b, 1, LRU_W), F32)),
        scratch_shapes=[pltpu.VMEM((chunk + 2 * SUBLANES, LRU_W), F32), pltpu.VMEM((chunk, LRU_W), F32),
                        pltpu.VMEM((chunk, LRU_W), F32), pltpu.VMEM((1, LRU_W), F32)],
        compiler_params=_cparams(("arbitrary", "arbitrary"), 32),
        name="rglru",
    )(x3, xg3, conv_buf, h0.reshape(b, 1, LRU_W), conv_w, vec, wa, wx)


def _layer_norm(x, g, b):
    mu = jnp.mean(x, axis=-1, keepdims=True)
    xc = x - mu
    var = jnp.mean(xc * xc, axis=-1, keepdims=True)
    return xc * lax.rsqrt(var + LN_EPS) * g + b


def _mix_kernel(ya_ref, ys_ref, yl_ref, gl_ref, x_ref, wbr_ref, wout_ref, vec_ref, wr_ref, br_ref,
                x1_ref, x1b_ref, comb_ref, pos_ref, cnt_ref, *, alpha, tiles_per_block):
    i = pl.program_id(0)
    tm = x_ref.shape[0]
    mixed = None
    for n, y_ref in enumerate((ya_ref, ys_ref, yl_ref)):
        proj = jnp.dot(y_ref[...].astype(BF16), wbr_ref[n], preferred_element_type=F32)
        t = jax.nn.sigmoid(gl_ref[:, n * D_MODEL:(n + 1) * D_MODEL]) * proj
        mixed = t if mixed is None else mixed + t
    mixed = jnp.dot(mixed.astype(BF16), wout_ref[...], preferred_element_type=F32) + vec_ref[0:1, :]
    x1 = _layer_norm(alpha * x_ref[...] + mixed, vec_ref[1:2, :], vec_ref[2:3, :])
    x1_ref[...] = x1
    x1b_ref[...] = x1.astype(BF16)

    logits = jnp.dot(x1.astype(BF16), wr_ref[...].astype(BF16), preferred_element_type=F32) + br_ref[...]
    lane = lax.broadcasted_iota(jnp.int32, (tm, N_EXPERTS), 1)
    work = logits
    vals, hits = [], []
    for _ in range(TOP_K):
        v = jnp.max(work, axis=1, keepdims=True)
        ix = jnp.min(jnp.where(work == v, lane, N_EXPERTS), axis=1, keepdims=True)
        hit = lane == ix
        vals.append(v)
        hits.append(hit)
        work = jnp.where(hit, -jnp.inf, work)
    es = [jnp.exp(v - vals[0]) for v in vals]
    den = es[0] + es[1] + es[2] + es[3]
    comb = jnp.zeros((tm, N_EXPERTS), F32)
    routed = jnp.zeros((tm, N_EXPERTS), F32)
    for e, hit in zip(es, hits):
        comb = comb + jnp.where(hit, e / den, 0.0)
        routed = routed + jnp.where(hit, 1.0, 0.0)
    comb_ref[...] = comb

    @pl.when(i % tiles_per_block == 0)
    def _():
        cnt_ref[...] = jnp.zeros(cnt_ref.shape, F32)

    before = (lax.broadcasted_iota(jnp.int32, (tm, tm), 1) < lax.broadcasted_iota(jnp.int32, (tm, tm), 0))
    rank = jnp.dot(jnp.where(before, 1.0, 0.0).astype(BF16), routed.astype(BF16), preferred_element_type=F32)
    pos = rank + cnt_ref[...]
    pos_ref[...] = jnp.where(routed > 0.5, pos, -1.0)
    cnt_ref[...] = cnt_ref[...] + jnp.sum(routed, axis=0, keepdims=True)


def _mix(ya, ys, yl, gl, x, w_branch, w_out, b_out, ln_g, ln_b, w_r, b_r, alpha, moe_block):
    n = x.shape[0]
    tm = min(256, n)
    row = lambda wd: pl.BlockSpec((tm, wd), lambda i: (i, 0))
    full = lambda a: pl.BlockSpec(a.shape, lambda i: (0,) * a.ndim)
    vec = jnp.concatenate([b_out[None], ln_g[None], ln_b[None], jnp.zeros((SUBLANES - 3, D_MODEL), F32)], 0)
    wbr = w_branch.astype(BF16)
    wout = w_out.astype(BF16)
    br = b_r.reshape(1, N_EXPERTS)
    return pl.pallas_call(
        functools.partial(_mix_kernel, alpha=alpha, tiles_per_block=moe_block // tm),
        grid=(n // tm,),
        in_specs=[row(ATT_W), row(SSM_W), row(LRU_W), row(N_BRANCH * D_MODEL), row(D_MODEL),
                  full(wbr), full(wout), full(vec), full(w_r), full(br)],
        out_specs=(row(D_MODEL), row(D_MODEL), row(N_EXPERTS), row(N_EXPERTS)),
        out_shape=(jax.ShapeDtypeStruct((n, D_MODEL), F32), jax.ShapeDtypeStruct((n, D_MODEL), BF16),
                   jax.ShapeDtypeStruct((n, N_EXPERTS), F32), jax.ShapeDtypeStruct((n, N_EXPERTS), F32)),
        scratch_shapes=[pltpu.VMEM((1, N_EXPERTS), F32)],
        compiler_params=_cparams(("arbitrary",), 48),
        name="mix_ln1_router",
    )(ya, ys, yl, gl, x, wbr, wout, vec, w_r, br)


MOE_BLOCK = 2048
MOE_CHUNK = 256
MOE_TAIL = 64
MOE_SCATTER = 512


def _moe_kernel(cnt_ref, xb_ref, post_ref, pos_ref, comb_ref, wgu_ref, bgu_ref, wdn_ref, bdn_ref, o_ref,
                *, ch, ch_tail, sb):
    tb_i = pl.program_id(0)
    e = pl.program_id(1)
    tb = xb_ref.shape[0]

    @pl.when(e == 0)
    def _():
        o_ref[...] = jnp.zeros(o_ref.shape, F32)

    cnt = cnt_ref[tb_i * N_EXPERTS + e]
    full = cnt // ch
    rem = cnt % ch
    short = jnp.logical_and(rem > 0, rem <= ch_tail)
    widen = jnp.logical_and(short, full > 0)
    tail_only = jnp.logical_and(short, full == 0)
    n_main = full + jnp.where(jnp.logical_and(rem > 0, jnp.logical_not(short)), 1, 0) - jnp.where(widen, 1, 0)
    lane_e = lax.broadcasted_iota(jnp.int32, (1, N_EXPERTS), 1) == e
    slot_row = post_ref[0]

    def run_chunk(base, ch):
        pick = (lax.broadcasted_iota(jnp.int32, (ch, tb), 0) + base).astype(F32) == slot_row
        xc = jnp.dot(jnp.where(pick, 1.0, 0.0).astype(BF16), xb_ref[...], preferred_element_type=F32).astype(BF16)
        gu = jnp.dot(xc, wgu_ref[0], preferred_element_type=F32) + bgu_ref[0]
        gate = jnp.minimum(gu[:, :D_FF], SWIGLU_LIMIT)
        up = jnp.clip(gu[:, D_FF:], -SWIGLU_LIMIT, SWIGLU_LIMIT)
        act = (up + 1.0) * gate * jax.nn.sigmoid(SWIGLU_ALPHA * gate)
        y = (jnp.dot(act.astype(BF16), wdn_ref[0], preferred_element_type=F32) + bdn_ref[0]).astype(BF16)
        for s in range(tb // sb):
            rows = slice(s * sb, (s + 1) * sb)
            slot_col = jnp.sum(jnp.where(lane_e, pos_ref[rows, :], 0.0), axis=1, keepdims=True)
            w_col = jnp.sum(jnp.where(lane_e, comb_ref[rows, :], 0.0), axis=1, keepdims=True)
            upd = None
            for k0 in range(0, ch, MOE_CHUNK):
                kw = min(MOE_CHUNK, ch - k0)
                put = (lax.broadcasted_iota(jnp.int32, (sb, kw), 1) + (base + k0)).astype(F32) == slot_col
                t = jnp.dot(jnp.where(put, w_col, 0.0).astype(BF16), y[k0:k0 + kw], preferred_element_type=F32)
                upd = t if upd is None else upd + t
            o_ref[rows, :] += upd

    def main_body(c, carry):
        run_chunk(c * ch, ch)
        return carry

    lax.fori_loop(0, n_main, main_body, 0)

    @pl.when(widen)
    def _():
        run_chunk(n_main * ch, ch + ch_tail)

    @pl.when(tail_only)
    def _():
        run_chunk(0, ch_tail)


def _moe(x1b, comb, pos, wgu, bgu, wdn, bdn):
    n = x1b.shape[0]
    tb = min(MOE_BLOCK, n)
    ch = min(MOE_CHUNK, tb)
    ch_tail = min(MOE_TAIL, ch)
    sb = min(MOE_SCATTER, tb)
    n_tb = n // tb
    routed = (pos >= 0).astype(jnp.int32)
    cnt = routed.reshape(n_tb, tb, N_EXPERTS).sum(axis=1).reshape(-1)
    pos_t = pos.T.reshape(N_EXPERTS, 1, n)
    grid_spec = pltpu.PrefetchScalarGridSpec(
        num_scalar_prefetch=1,
        grid=(n_tb, N_EXPERTS),
        in_specs=[pl.BlockSpec((tb, D_MODEL), lambda t, e, c: (t, 0)),
                  pl.BlockSpec((1, 1, tb), lambda t, e, c: (e, 0, t)),
                  pl.BlockSpec((tb, N_EXPERTS), lambda t, e, c: (t, 0)),
                  pl.BlockSpec((tb, N_EXPERTS), lambda t, e, c: (t, 0)),
                  pl.BlockSpec((1, D_MODEL, 2 * D_FF), lambda t, e, c: (e, 0, 0)),
                  pl.BlockSpec((1, 1, 2 * D_FF), lambda t, e, c: (e, 0, 0)),
                  pl.BlockSpec((1, D_FF, D_MODEL), lambda t, e, c: (e, 0, 0)),
                  pl.BlockSpec((1, 1, D_MODEL), lambda t, e, c: (e, 0, 0))],
        out_specs=pl.BlockSpec((tb, D_MODEL), lambda t, e, c: (t, 0)),
    )
    return pl.pallas_call(
        functools.partial(_moe_kernel, ch=ch, ch_tail=ch_tail, sb=sb),
        grid_spec=grid_spec,
        out_shape=jax.ShapeDtypeStruct((n, D_MODEL), F32),
        compiler_params=_cparams(("arbitrary", "arbitrary"), 56),
        name="moe",
    )(cnt, x1b, pos_t, pos, comb, wgu, bgu, wdn, bdn)


def _ln2_kernel(x_ref, m_ref, vec_ref, o_ref, *, alpha):
    o_ref[...] = _layer_norm(alpha * x_ref[...] + m_ref[...], vec_ref[0:1, :], vec_ref[1:2, :])


def _ln2(x1, moe_out, g, b, alpha):
    n = x1.shape[0]
    tm = min(512, n)
    vec = jnp.concatenate([g[None], b[None], jnp.zeros((SUBLANES - 2, D_MODEL), F32)], 0)
    row = pl.BlockSpec((tm, D_MODEL), lambda i: (i, 0))
    return pl.pallas_call(
        functools.partial(_ln2_kernel, alpha=alpha),
        grid=(n // tm,),
        in_specs=[row, row, pl.BlockSpec(vec.shape, lambda i: (0, 0))],
        out_specs=row,
        out_shape=jax.ShapeDtypeStruct((n, D_MODEL), F32),
        compiler_params=_cparams(("arbitrary",), 32),
        name="ln2",
    )(x1, moe_out, vec)


def _pad_rows(a, rows):
    pad = [(0, 0)] * a.ndim
    pad[1] = (0, rows - a.shape[1])
    return jnp.pad(a, pad)


def kernel(x_prompt, x_sample, cache_k, cache_v, cache_kidx, state_ssm_re, state_ssm_im, state_lru_h, state_lru_conv, page_table, w_in, b_in, ssm_lam_re, ssm_lam_im, ssm_log_dt, ssm_b_re, ssm_b_im, ssm_c_re, ssm_c_im, ssm_d, ssm_w_glu, ssm_b_glu, lru_conv_w, lru_conv_b, lru_w_a, lru_b_a, lru_w_x, lru_b_x, lru_lam, w_branch, w_out, b_out, ln1_g, ln1_b, moe_w_r, moe_b_r, moe_w_gu, moe_b_gu, moe_w_dn, moe_b_dn, ln2_g, ln2_b):
    depth = w_in.shape[0]
    alpha = (2 * depth) ** 0.25
    bp, tp, _ = x_prompt.shape
    db, ns, _ = x_sample.shape
    assert ns <= SAMPLE_Q
    n_pages = page_table.shape[1]
    past = n_pages * PAGE_SIZE
    pt_flat = page_table.reshape(-1).astype(jnp.int32)
    cache_kt = jnp.transpose(cache_k, (0, 1, 3, 4, 2))
    cache_vt = jnp.transpose(cache_v, (0, 1, 3, 4, 2))
    cache_kidx_t = jnp.transpose(cache_kidx, (0, 1, 3, 2))
    topk_p = min(TOPK_MAX, tp // 4)
    topk_s = min(TOPK_MAX, (past + ns) // 4)

    q_end = 3 * ATT_W
    i_end = q_end + IDX_KI + IDX_DIM + IDX_HEADS

    def regroup(a):
        padw = [(0, 0)] * (a.ndim - 1) + [(0, IDX_PAD - (i_end - q_end))]
        return jnp.concatenate([a[..., :q_end], jnp.pad(a[..., q_end:i_end], padw), a[..., i_end:]], axis=-1)

    w_in_r = regroup(w_in).astype(BF16)
    b_in_r = regroup(b_in).reshape(depth, 1, N_IN_PAD)
    wgu_b = moe_w_gu.astype(BF16)
    wdn_b = moe_w_dn.astype(BF16)
    bgu3 = moe_b_gu.reshape(depth, N_EXPERTS, 1, 2 * D_FF)
    bdn3 = moe_b_dn.reshape(depth, N_EXPERTS, 1, D_MODEL)

    xp = x_prompt.reshape(bp * tp, D_MODEL)
    xs = x_sample.reshape(db * ns, D_MODEL)
    new_p = [[] for _ in range(7)]
    new_s = [[] for _ in range(7)]

    def channel_mix(l, x, ya, ys, yl, gl):
        n = x.shape[0]
        x1, x1b, comb, pos = _mix(ya, ys, yl, gl, x, w_branch[l], w_out[l], b_out[l], ln1_g[l], ln1_b[l],
                                  moe_w_r[l], moe_b_r[l], alpha, min(MOE_BLOCK, n))
        moe_out = _moe(x1b, comb, pos, wgu_b[l], bgu3[l], wdn_b[l], bdn3[l])
        return _ln2(x1, moe_out, ln2_g[l], ln2_b[l], alpha)

    for l in range(depth):
        ssm_p = (ssm_lam_re[l], ssm_lam_im[l], ssm_log_dt[l], ssm_b_re[l], ssm_b_im[l], ssm_c_re[l],
                 ssm_c_im[l], ssm_d[l], ssm_w_glu[l], ssm_b_glu[l])
        lru_p = (lru_conv_w[l], lru_conv_b[l], lru_w_a[l], lru_b_a[l], lru_w_x[l], lru_b_x[l], lru_lam[l])

        q_hm, k, k_hm, v, vt_hm, idx, u, xr, xg, gl = _in_proj(xp, w_in_r[l], b_in_r[l])
        r3 = lambda a: a.reshape(bp, tp, a.shape[-1])
        idx3 = r3(idx)
        wi_t = _pad_rows(jnp.swapaxes(idx3[:, :, IDX_WI:IDX_WI + IDX_HEADS], 1, 2), SUBLANES)
        bias_t = _dsa_index_t(idx3, wi_t, topk_p)
        ya = jnp.swapaxes(_dsa_attn_t(q_hm, k_hm, vt_hm, bias_t), 1, 2).reshape(bp * tp, ATT_W)
        zeros_n = jnp.zeros((bp, SSM_N), F32)
        ys, s_re, s_im = _s5(r3(u), zeros_n, zeros_n, tp, *ssm_p)
        yl, nbuf, hl = _rglru(r3(xr), r3(xg), jnp.zeros((bp, CONV_W - 1, LRU_W), F32), jnp.zeros((bp, LRU_W), F32),
                              tp, *lru_p)
        st = (k.reshape(bp, tp, N_HEADS, HEAD_DIM), v.reshape(bp, tp, N_HEADS, HEAD_DIM),
              idx3[:, :, IDX_KI:IDX_KI + IDX_DIM], s_re.reshape(bp, SSM_GROUPS, SSM_STATE),
              s_im.reshape(bp, SSM_GROUPS, SSM_STATE), nbuf, hl.reshape(bp, LRU_W))
        for jj in range(7):
            new_p[jj].append(st[jj])
        xp = channel_mix(l, xp, ya, ys.reshape(bp * tp, SSM_W), yl.reshape(bp * tp, LRU_W), gl)

        q_hm, k, _, v, _, idx, u, xr, xg, gl = _in_proj(xs, w_in_r[l], b_in_r[l])
        r3 = lambda a: a.reshape(db, ns, a.shape[-1])
        idx3 = r3(idx)
        ki_new = idx3[:, :, IDX_KI:IDX_KI + IDX_DIM]
        keep = _sample_index(pt_flat, _pad_rows(idx3, SAMPLE_ROWS), _pad_rows(ki_new, PAGE_SIZE), cache_kidx_t, l,
                             n_pages, topk_s, ns)
        q_rows = jnp.transpose(q_hm.reshape(N_HEADS, db, ns, HEAD_DIM), (1, 2, 0, 3)).reshape(db, ns, ATT_W)
        ya = _sample_attn_t(pt_flat, _pad_rows(q_rows, SAMPLE_ROWS), keep, _pad_rows(r3(k).astype(BF16), PAGE_SIZE),
                            _pad_rows(r3(v).astype(BF16), PAGE_SIZE), cache_kt, cache_vt, l, n_pages)
        ya = ya[:, :ns].reshape(db * ns, ATT_W)
        ys, s_re, s_im = _s5(_pad_rows(r3(u), SAMPLE_ROWS), state_ssm_re[l].reshape(db, SSM_N),
                             state_ssm_im[l].reshape(db, SSM_N), ns, *ssm_p)
        yl, nbuf, hl = _rglru(_pad_rows(r3(xr), SAMPLE_ROWS), _pad_rows(r3(xg), SAMPLE_ROWS), state_lru_conv[l],
                              state_lru_h[l], ns, *lru_p)
        st = (k.reshape(db, ns, N_HEADS, HEAD_DIM), v.reshape(db, ns, N_HEADS, HEAD_DIM), ki_new,
              s_re.reshape(db, SSM_GROUPS, SSM_STATE), s_im.reshape(db, SSM_GROUPS, SSM_STATE), nbuf,
              hl.reshape(db, LRU_W))
        for jj in range(7):
            new_s[jj].append(st[jj])
        xs = channel_mix(l, xs, ya, ys[:, :ns].reshape(db * ns, SSM_W), yl[:, :ns].reshape(db * ns, LRU_W), gl)

    k_p, v_p, ki_p, sre_p, sim_p, lc_p, lh_p = [jnp.stack(a) for a in new_p]
    k_s, v_s, ki_s, sre_s, sim_s, lc_s, lh_s = [jnp.stack(a) for a in new_s]
    return (xp.reshape(bp, tp, D_MODEL), xs.reshape(db, ns, D_MODEL), k_p, v_p, ki_p, sre_p, sim_p, lh_p, lc_p,
            k_s, v_s, ki_s, sre_s, sim_s, lh_s, lc_s)
```

```python
import functools
import math

import jax
import jax.numpy as jnp
from jax import lax
from jax.experimental import pallas as pl
from jax.experimental.pallas import tpu as pltpu

F32 = jnp.float32
BF16 = jnp.bfloat16

D_MODEL = 1024
N_HEADS = 8
HEAD_DIM = 64
ATT_W = N_HEADS * HEAD_DIM
IDX_HEADS = 4
IDX_DIM = 64
TOPK_MAX = 256
PAGE_SIZE = 128
SSM_W = 512
SSM_GROUPS = 32
SSM_GROUP = 16
SSM_STATE = 64
SSM_N = SSM_GROUPS * SSM_STATE
LRU_W = 512
CONV_W = 4
LRU_C = 8.0
N_EXPERTS = 32
TOP_K = 4
D_FF = 1024
SWIGLU_LIMIT = 7.0
SWIGLU_ALPHA = 1.702
N_BRANCH = 3
LN_EPS = 1e-5
IDX_PAD = 384
IDX_KI = IDX_HEADS * IDX_DIM
IDX_WI = IDX_KI + IDX_DIM
NEG_BIG = -1e30
M_INIT = -1e29
QK_SCALE = HEAD_DIM ** -0.5 * math.log2(math.e)
SUBLANES = 8
LANES = 128


def _cparams(sem, vmem_mb):
    return pltpu.CompilerParams(dimension_semantics=sem, vmem_limit_bytes=vmem_mb * 1024 * 1024)


_SEG_W = (ATT_W, ATT_W, ATT_W, IDX_PAD, SSM_W, LRU_W, LRU_W, N_BRANCH * D_MODEL)
_SEG_OFF = tuple(sum(_SEG_W[:i]) for i in range(len(_SEG_W)))
N_IN_PAD = sum(_SEG_W)


def _in_proj_kernel(x_ref, w_ref, b_ref, wvt_ref, bvt_ref, q_ref, k_ref, kb_ref, v_ref, vt_ref, idx_ref, u_ref,
                    xr_ref, xg_ref, gl_ref):
    xb = x_ref[...].astype(BF16)

    def seg(i):
        off, wd = _SEG_OFF[i], _SEG_W[i]
        return jnp.dot(xb, w_ref[:, off:off + wd], preferred_element_type=F32) + b_ref[:, off:off + wd]

    tm = x_ref.shape[0]
    q = seg(0) * QK_SCALE
    k = seg(1)
    v = seg(2)
    k_ref[...] = k
    v_ref[...] = v
    vt = lax.dot_general(wvt_ref[...], xb, (((1,), (1,)), ((), ())), preferred_element_type=F32) + bvt_ref[...]
    ones = jnp.ones((HEAD_DIM, tm), F32)
    for h in range(N_HEADS):
        sl = slice(h * HEAD_DIM, (h + 1) * HEAD_DIM)
        q_ref[h] = q[:, sl].astype(BF16)
        kb_ref[h] = k[:, sl].astype(BF16)
        vt_ref[h] = jnp.concatenate([vt[sl, :], ones], axis=0).astype(BF16)
    idx_ref[...] = seg(3)
    u_ref[...] = seg(4)
    xr_ref[...] = seg(5)
    xg_ref[...] = seg(6)
    gl_ref[...] = seg(7)


def _in_proj(x, w, b):
    n = x.shape[0]
    tm = min(256, n)
    row = lambda wd: pl.BlockSpec((tm, wd), lambda i: (i, 0))
    full = lambda a: pl.BlockSpec(a.shape, lambda i: (0,) * a.ndim)
    heads = lambda wd: pl.BlockSpec((N_HEADS, tm, wd), lambda i: (0, i, 0))
    v_off = _SEG_OFF[2]
    wvt = w[:, v_off:v_off + ATT_W].T
    bvt = b[:, v_off:v_off + ATT_W].T
    out_shape = (
        jax.ShapeDtypeStruct((N_HEADS, n, HEAD_DIM), BF16),
        jax.ShapeDtypeStruct((n, ATT_W), F32),
        jax.ShapeDtypeStruct((N_HEADS, n, HEAD_DIM), BF16),
        jax.ShapeDtypeStruct((n, ATT_W), F32),
        jax.ShapeDtypeStruct((N_HEADS, 2 * HEAD_DIM, n), BF16),
        jax.ShapeDtypeStruct((n, IDX_PAD), F32),
        jax.ShapeDtypeStruct((n, SSM_W), F32),
        jax.ShapeDtypeStruct((n, LRU_W), F32),
        jax.ShapeDtypeStruct((n, LRU_W), F32),
        jax.ShapeDtypeStruct((n, N_BRANCH * D_MODEL), F32),
    )
    return pl.pallas_call(
        _in_proj_kernel,
        grid=(n // tm,),
        in_specs=[row(D_MODEL), full(w), full(b), full(wvt), full(bvt)],
        out_specs=tuple(pl.BlockSpec((N_HEADS, 2 * HEAD_DIM, tm), lambda i: (0, 0, i)) if s.shape[2:] == (n,)
                        else heads(s.shape[2]) if len(s.shape) == 3 else row(s.shape[1]) for s in out_shape),
        out_shape=out_shape,
        compiler_params=_cparams(("arbitrary",), 56),
        name="in_proj",
    )(x, w, b, wvt, bvt)


BISECT_STEPS = 20


def _prefix_matrix(ck):
    r = lax.broadcasted_iota(jnp.int32, (ck, ck), 0)
    c = lax.broadcasted_iota(jnp.int32, (ck, ck), 1)
    return jnp.where(r <= c, 1.0, 0.0).astype(BF16)


def _select_rows(s_ref, rows, rb, n_chunks, ck, kk, n_adm):
    nl = ck // LANES

    def tile(c, j):
        return s_ref[rows, pl.ds(pl.multiple_of(c * ck + j * LANES, LANES), LANES)]

    def lanes(x):
        return jnp.broadcast_to(x, (rb, LANES))

    def count_gt(x):
        xb = lanes(x)

        def body(c, acc):
            for j in range(nl):
                acc = acc + lax.shift_right_arithmetic(pltpu.bitcast(xb - tile(c, j), jnp.int32), 31)
            return acc
        neg = lax.fori_loop(0, n_chunks, body, jnp.zeros((rb, LANES), jnp.int32))
        return -jnp.sum(neg.astype(F32), axis=1, keepdims=True)

    def min_above_max_upto(lo, hi):
        lob, hib = lanes(lo), lanes(hi)

        def body(c, acc):
            mn, mx = acc
            for j in range(nl):
                v = tile(c, j)
                mn = jnp.minimum(mn, jnp.where(v > lob, v, jnp.inf))
                mx = jnp.maximum(mx, jnp.where(v <= hib, v, -jnp.inf))
            return mn, mx
        mn, mx = lax.fori_loop(0, n_chunks, body, (jnp.full((rb, LANES), jnp.inf, F32),
                                                   jnp.full((rb, LANES), -jnp.inf, F32)))
        return jnp.min(mn, axis=1, keepdims=True), jnp.max(mx, axis=1, keepdims=True)

    def bisect_at(x, st):
        lo, hi, clo, chi = st
        c = count_gt(x)
        ge = c >= kk
        return (jnp.where(ge, x, lo), jnp.where(ge, hi, x), jnp.where(ge, c, clo), jnp.where(ge, chi, c))

    inf = jnp.full((rb, 1), jnp.inf, F32)
    row_min, row_max = min_above_max_upto(-inf, inf)
    st = (row_min - 1.0 - jnp.abs(row_min), row_max, n_adm, jnp.zeros((rb, 1), F32))
    st = lax.fori_loop(0, BISECT_STEPS, lambda _, s: bisect_at(0.5 * s[0] + 0.5 * s[1], s), st)

    def refine_cond(carry):
        return carry[3] > 0.5

    def refine(carry):
        st, a, b, _ = carry
        x = 0.5 * a + 0.5 * b
        st = bisect_at(jnp.where(x >= b, a, x), st)
        a, b = min_above_max_upto(st[0], st[1])
        return st, a, b, jnp.max(jnp.where(a == b, 0.0, 1.0))

    a, b = min_above_max_upto(st[0], st[1])
    st, tau, _, _ = lax.while_loop(refine_cond, refine, (st, a, b, jnp.max(jnp.where(a == b, 0.0, 1.0))))

    return tau, kk - st[3]


def _emit_selection(s_ref, n_chunks, ck, tau, need, emit):
    rows = s_ref.shape[0]
    tri = _prefix_matrix(ck)

    def body(c, seen):
        v = s_ref[:, pl.ds(pl.multiple_of(c * ck, LANES), ck)]
        eq = v == tau
        pref = jnp.dot(jnp.where(eq, 1.0, 0.0).astype(BF16), tri, preferred_element_type=F32) + seen
        emit(c, (v > tau) | (eq & (pref <= need)))
        return pref[:, ck - 1:ck]

    lax.fori_loop(0, n_chunks, body, jnp.zeros((rows, 1), F32))


DSA_TQ = 256
DSA_CK = 512

N_ACC = 4
ATT_TQ = 512
ATT_TK = 1024


def _select_cols(s_ref, cols, n_chunks, ck, kk, n_adm):
    groups = ck // SUBLANES

    def bc(x):
        return jnp.broadcast_to(x, (SUBLANES, LANES))

    def reduce_keys(fn, op, init, dtype):
        def body(c, accs):
            accs = list(accs)
            slab = s_ref.at[pl.ds(pl.multiple_of(c * ck, ck), ck), cols]
            for g in range(groups):
                accs[g % N_ACC] = op(accs[g % N_ACC], fn(slab[g * SUBLANES:(g + 1) * SUBLANES, :]))
            return tuple(accs)
        accs = lax.fori_loop(0, n_chunks, body, tuple(jnp.full((SUBLANES, LANES), init, dtype) for _ in range(N_ACC)))
        out = accs[0]
        for a in accs[1:]:
            out = op(out, a)
        return out

    def count_gt(x):
        xb = bc(x)
        neg = reduce_keys(lambda v: lax.shift_right_arithmetic(pltpu.bitcast(xb - v, jnp.int32), 31), jnp.add, 0,
                          jnp.int32)
        return -jnp.sum(neg.astype(F32), axis=0, keepdims=True)

    def min_above(lo):
        lob = bc(lo)
        return jnp.min(reduce_keys(lambda v: jnp.where(v > lob, v, jnp.inf), jnp.minimum, jnp.inf, F32),
                       axis=0, keepdims=True)

    def max_upto(hi):
        hib = bc(hi)
        return jnp.max(reduce_keys(lambda v: jnp.where(v <= hib, v, -jnp.inf), jnp.maximum, -jnp.inf, F32),
                       axis=0, keepdims=True)

    def bisect_at(x, st):
        lo, hi, clo, chi = st
        c = count_gt(x)
        ge = c >= kk
        return (jnp.where(ge, x, lo), jnp.where(ge, hi, x), jnp.where(ge, c, clo), jnp.where(ge, chi, c))

    inf = jnp.full((1, LANES), jnp.inf, F32)
    col_min, col_max = min_above(-inf), max_upto(inf)
    st = (col_min - 1.0 - jnp.abs(col_min), col_max, n_adm, jnp.zeros((1, LANES), F32))
    st = lax.fori_loop(0, BISECT_STEPS, lambda _, s: bisect_at(0.5 * s[0] + 0.5 * s[1], s), st)

    def refine(carry):
        st, a, b, _ = carry
        x = 0.5 * a + 0.5 * b
        st = bisect_at(jnp.where(x >= b, a, x), st)
        a, b = min_above(st[0]), max_upto(st[1])
        return st, a, b, jnp.max(jnp.where(a == b, 0.0, 1.0))

    a, b = min_above(st[0]), max_upto(st[1])
    st, tau, _, _ = lax.while_loop(lambda carry: carry[3] > 0.5, refine,
                                   (st, a, b, jnp.max(jnp.where(a == b, 0.0, 1.0))))
    return tau, kk - st[3]


def _dsa_index_kernel_t(q_ref, wi_ref, kw_ref, bias_ref, s_ref, tau_ref, need_ref, *, topk):
    t_len, tq = s_ref.shape
    ck = min(DSA_CK, t_len)
    i = pl.program_id(1)
    t0 = i * tq
    n_valid = ((i + 1) * tq + ck - 1) // ck
    qi = q_ref[0, :, 0:IDX_KI].astype(BF16)
    wi = wi_ref[0]
    t_ids = t0 + lax.broadcasted_iota(jnp.int32, (1, tq), 1)
    key_row = lax.broadcasted_iota(jnp.int32, (ck, 1), 0)

    def score_body(c, carry):
        off = pl.multiple_of(c * ck, ck)
        keys = kw_ref[0, pl.ds(off, ck), 0:IDX_DIM].astype(BF16)
        sc = None
        for h in range(IDX_HEADS):
            d = lax.dot_general(keys, qi[:, h * IDX_DIM:(h + 1) * IDX_DIM], (((1,), (1,)), ((), ())),
                                preferred_element_type=F32)
            t = wi[h:h + 1, :] * jnp.maximum(d, 0.0)
            sc = t if sc is None else sc + t
        s_ref[pl.ds(off, ck), :] = jnp.where(key_row + c * ck <= t_ids, sc, -jnp.inf)
        return carry

    lax.fori_loop(0, n_valid, score_body, 0)

    for lb in range(tq // LANES):
        cols = slice(lb * LANES, (lb + 1) * LANES)
        n_adm = (t0 + lb * LANES + 1 + lax.broadcasted_iota(jnp.int32, (1, LANES), 1)).astype(F32)
        tau, need = _select_cols(s_ref, cols, n_valid, ck, jnp.minimum(n_adm, float(topk)), n_adm)
        tau_ref[:, cols] = tau
        need_ref[:, cols] = need

    tau = tau_ref[...]
    need = need_ref[...]
    before = (lax.broadcasted_iota(jnp.int32, (ck, ck), 1) <= lax.broadcasted_iota(jnp.int32, (ck, ck), 0))
    tri = jnp.where(before, 1.0, 0.0).astype(BF16)

    def emit_body(c, seen):
        off = pl.multiple_of(c * ck, ck)
        v = s_ref[pl.ds(off, ck), :]
        eq = v == tau
        pref = jnp.dot(tri, jnp.where(eq, 1.0, 0.0).astype(BF16), preferred_element_type=F32) + seen
        keep = (v > tau) | (eq & (pref <= need))
        bias_ref[0, pl.ds(off, ck), :] = jnp.where(keep, 0.0, NEG_BIG).astype(BF16)
        return pref[ck - 1:ck, :]

    lax.fori_loop(0, n_valid, emit_body, jnp.zeros((1, tq), F32))

    def fill_body(c, carry):
        bias_ref[0, pl.ds(pl.multiple_of(c * ck, ck), ck), :] = jnp.full((ck, tq), NEG_BIG, BF16)
        return carry

    lax.fori_loop(n_valid, t_len // ck, fill_body, 0)


def _dsa_index_t(idx3, wi_t, topk):
    b, t, _ = idx3.shape
    tq = min(DSA_TQ, t)
    return pl.pallas_call(
        functools.partial(_dsa_index_kernel_t, topk=topk),
        grid=(b, t // tq),
        in_specs=[pl.BlockSpec((1, tq, IDX_PAD), lambda bi, i: (bi, i, 0)),
                  pl.BlockSpec((1, SUBLANES, tq), lambda bi, i: (bi, 0, i)),
                  pl.BlockSpec((1, t, LANES), lambda bi, i: (bi, 0, IDX_KI // LANES))],
        out_specs=pl.BlockSpec((1, t, tq), lambda bi, i: (bi, 0, i)),
        out_shape=jax.ShapeDtypeStruct((b, t, t), BF16),
        scratch_shapes=[pltpu.VMEM((t, tq), F32), pltpu.VMEM((1, tq), F32), pltpu.VMEM((1, tq), F32)],
        compiler_params=_cparams(("arbitrary", "arbitrary"), 48),
        name="dsa_index",
    )(idx3, wi_t, idx3)


def _dsa_attn_kernel_t(q_ref, k_ref, vt_ref, bias_ref, o_ref, bias_scr, *state):
    m_refs, acc_refs = state[:N_HEADS], state[N_HEADS:]
    tq = q_ref.shape[1]
    tk = k_ref.shape[1]
    i = pl.program_id(1)
    j = pl.program_id(2)
    last = ((i + 1) * tq - 1) // tk

    @pl.when(j == 0)
    def _():
        for h in range(N_HEADS):
            m_refs[h][...] = jnp.full(m_refs[h].shape, M_INIT, F32)
            acc_refs[h][...] = jnp.zeros(acc_refs[h].shape, F32)

    @pl.when(j <= last)
    def _():
        bias = bias_ref[0].astype(F32)

        def qk(h):
            return lax.dot_general(k_ref[h], q_ref[h], (((1,), (1,)), ((), ())), preferred_element_type=F32)

        s_next = qk(0)
        for h in range(N_HEADS):
            s = s_next + bias
            if h + 1 < N_HEADS:
                s_next = qk(h + 1)
            m_prev = m_refs[h][...]
            m_new = jnp.maximum(m_prev, jnp.max(s, axis=0, keepdims=True))
            p = jnp.exp2(s - m_new)
            acc_refs[h][...] = (jnp.exp2(m_prev - m_new) * acc_refs[h][...]
                                + jnp.dot(vt_ref[h], p.astype(BF16), preferred_element_type=F32))
            m_refs[h][...] = m_new

    @pl.when(j == pl.num_programs(2) - 1)
    def _():
        for h in range(N_HEADS):
            acc = acc_refs[h][...]
            o_ref[0, h * HEAD_DIM:(h + 1) * HEAD_DIM, :] = acc[0:HEAD_DIM, :] / acc[HEAD_DIM:HEAD_DIM + 1, :]


def _dsa_attn_t(q_hm, k_hm, vt_hm, bias_t):
    b, t, _ = bias_t.shape
    tq = min(ATT_TQ, t)
    tk = min(ATT_TK, t)
    nq, nk = t // tq, t // tk
    last = lambda i: ((i + 1) * tq - 1) // tk
    return pl.pallas_call(
        _dsa_attn_kernel_t,
        grid=(b, nq, nk),
        in_specs=[pl.BlockSpec((N_HEADS, tq, HEAD_DIM), lambda bi, i, j: (0, bi * nq + i, 0)),
                  pl.BlockSpec((N_HEADS, tk, HEAD_DIM), lambda bi, i, j: (0, bi * nk + jnp.minimum(j, last(i)), 0)),
                  pl.BlockSpec((N_HEADS, 2 * HEAD_DIM, tk), lambda bi, i, j: (0, 0, bi * nk + jnp.minimum(j, last(i)))),
                  pl.BlockSpec((1, tk, tq), lambda bi, i, j: (bi, jnp.minimum(j, last(i)), i))],
        out_specs=pl.BlockSpec((1, ATT_W, tq), lambda bi, i, j: (bi, 0, i)),
        out_shape=jax.ShapeDtypeStruct((b, ATT_W, t), F32),
        scratch_shapes=[pltpu.VMEM((tk, tq), F32)] + [pltpu.VMEM((1, tq), F32)] * N_HEADS
                       + [pltpu.VMEM((2 * HEAD_DIM, tq), F32)] * N_HEADS,
        compiler_params=_cparams(("arbitrary", "arbitrary", "arbitrary"), 32),
        name="dsa_attn",
    )(q_hm, k_hm, vt_hm, bias_t)


PAGES_PER_STEP = 16
SAMPLE_ROWS = 8
SAMPLE_CK = 640


def _sample_index_kernel(pt_ref, q_ref, knew_ref, *refs, topk, n_new):
    pages = refs[:PAGES_PER_STEP]
    keep_ref, s_ref = refs[PAGES_PER_STEP:]
    j = pl.program_id(1)
    width = s_ref.shape[1]
    past = width - PAGE_SIZE
    step_w = PAGES_PER_STEP * PAGE_SIZE
    q = q_ref[0]
    wi = q[:, IDX_WI:IDX_WI + IDX_HEADS]
    q4 = jnp.concatenate([q[:, h * IDX_DIM:(h + 1) * IDX_DIM] for h in range(IDX_HEADS)], axis=0).astype(BF16)

    def weighted(d):
        sc = None
        for h in range(IDX_HEADS):
            t = wi[:, h:h + 1] * jnp.maximum(d[h * SAMPLE_ROWS:(h + 1) * SAMPLE_ROWS], 0.0)
            sc = t if sc is None else sc + t
        return sc

    s_ref[:, pl.ds(pl.multiple_of(j * step_w, LANES), step_w)] = jnp.concatenate(
        [weighted(jnp.dot(q4, p[...].astype(BF16), preferred_element_type=F32)) for p in pages], axis=1)

    @pl.when(j == pl.num_programs(1) - 1)
    def _():
        sc = weighted(lax.dot_general(q4, knew_ref[0].astype(BF16), (((1,), (1,)), ((), ())),
                                      preferred_element_type=F32))
        row = lax.broadcasted_iota(jnp.int32, (SAMPLE_ROWS, PAGE_SIZE), 0)
        lane = lax.broadcasted_iota(jnp.int32, (SAMPLE_ROWS, PAGE_SIZE), 1)
        ok = (lane <= row) & (lane < n_new)
        s_ref[:, past:past + PAGE_SIZE] = jnp.where(ok, sc, -jnp.inf)
        r1 = lax.broadcasted_iota(jnp.int32, (SAMPLE_ROWS, 1), 0)
        n_adm = (past + jnp.minimum(r1 + 1, n_new)).astype(F32)

        def emit(c, keep):
            keep_ref[0, :, pl.ds(pl.multiple_of(c * SAMPLE_CK, LANES), SAMPLE_CK)] = jnp.where(keep, 1.0, 0.0).astype(BF16)

        tau, need = _select_rows(s_ref, pl.ds(0, SAMPLE_ROWS), SAMPLE_ROWS, width // SAMPLE_CK, SAMPLE_CK,
                                 jnp.full((SAMPLE_ROWS, 1), float(topk), F32), n_adm)
        _emit_selection(s_ref, width // SAMPLE_CK, SAMPLE_CK, tau, need, emit)


def _page_specs(layer, n_pages, block):
    def spec(r):
        def index_map(b, j, pt):
            return (layer, pt[b * n_pages + j * PAGES_PER_STEP + r]) + (0,) * len(block)
        return pl.BlockSpec((None, None) + tuple(block), index_map)

    return [spec(r) for r in range(PAGES_PER_STEP)]


def _sample_index(pt_flat, idx_pad, knew_pad, cache_kidx, layer, n_pages, topk, n_new):
    db = idx_pad.shape[0]
    width = n_pages * PAGE_SIZE + PAGE_SIZE
    assert width % SAMPLE_CK == 0 and n_pages % PAGES_PER_STEP == 0
    grid_spec = pltpu.PrefetchScalarGridSpec(
        num_scalar_prefetch=1,
        grid=(db, n_pages // PAGES_PER_STEP),
        in_specs=[pl.BlockSpec((1, SAMPLE_ROWS, IDX_PAD), lambda b, j, pt: (b, 0, 0)),
                  pl.BlockSpec((1, PAGE_SIZE, IDX_DIM), lambda b, j, pt: (b, 0, 0))]
                 + _page_specs(layer, n_pages, (IDX_DIM, PAGE_SIZE)),
        out_specs=pl.BlockSpec((1, SAMPLE_ROWS, width), lambda b, j, pt: (b, 0, 0)),
        scratch_shapes=[pltpu.VMEM((SAMPLE_ROWS, width), F32)],
    )
    return pl.pallas_call(
        functools.partial(_sample_index_kernel, topk=topk, n_new=n_new),
        grid_spec=grid_spec,
        out_shape=jax.ShapeDtypeStruct((db, SAMPLE_ROWS, width), BF16),
        compiler_params=_cparams(("arbitrary", "arbitrary"), 32),
        name="sample_index",
    )(pt_flat, idx_pad, knew_pad, *([cache_kidx] * PAGES_PER_STEP))


SAMPLE_Q = 4
SAMPLE_LINES = SAMPLE_Q * N_HEADS


def _sample_attn_kernel_t(pt_ref, q_ref, keep_ref, keepn_ref, knew_ref, vnew_ref, *refs):
    kp = refs[:PAGES_PER_STEP]
    vp = refs[PAGES_PER_STEP:2 * PAGES_PER_STEP]
    o_ref, m_ref, l_ref, acc_ref = refs[2 * PAGES_PER_STEP:]
    j = pl.program_id(1)
    own = (lax.broadcasted_iota(jnp.int32, (N_HEADS, ATT_W), 1) // HEAD_DIM
           == lax.broadcasted_iota(jnp.int32, (N_HEADS, ATT_W), 0))

    @pl.when(j == 0)
    def _():
        m_ref[...] = jnp.full(m_ref.shape, M_INIT, F32)
        l_ref[...] = jnp.zeros(l_ref.shape, F32)
        acc_ref[...] = jnp.zeros(acc_ref.shape, F32)

    q = q_ref[0].astype(F32)
    qbd = jnp.concatenate([jnp.where(own, jnp.broadcast_to(q[r:r + 1], (N_HEADS, ATT_W)), 0.0)
                           for r in range(SAMPLE_Q)], axis=0).astype(BF16)

    def update(s, keep8, pv):
        kf = keep8.astype(F32)
        kl = jnp.concatenate([jnp.broadcast_to(kf[r:r + 1], (N_HEADS, kf.shape[1])) for r in range(SAMPLE_Q)], axis=0)
        s = jnp.where(kl > 0.5, s, NEG_BIG)
        m_prev = m_ref[...]
        m_new = jnp.maximum(m_prev, jnp.max(s, axis=1, keepdims=True))
        alpha = jnp.exp2(m_prev - m_new)
        p = jnp.exp2(s - m_new)
        l_ref[...] = alpha * l_ref[...] + jnp.sum(p, axis=1, keepdims=True)
        acc_ref[...] = alpha * acc_ref[...] + pv(p.astype(BF16))
        m_ref[...] = m_new

    def page(ref):
        return ref[...].reshape(ATT_W, PAGE_SIZE).astype(BF16)

    s = jnp.concatenate([jnp.dot(qbd, page(kp[r]), preferred_element_type=F32) for r in range(PAGES_PER_STEP)], axis=1)

    def pv_pages(p):
        out = None
        for r in range(PAGES_PER_STEP):
            t = lax.dot_general(p[:, r * PAGE_SIZE:(r + 1) * PAGE_SIZE], page(vp[r]), (((1,), (1,)), ((), ())),
                                preferred_element_type=F32)
            out = t if out is None else out + t
        return out

    update(s, keep_ref[0], pv_pages)

    @pl.when(j == pl.num_programs(1) - 1)
    def _():
        s_new = lax.dot_general(qbd, knew_ref[0], (((1,), (1,)), ((), ())), preferred_element_type=F32)
        update(s_new, keepn_ref[0], lambda p: jnp.dot(p, vnew_ref[0], preferred_element_type=F32))
        o = acc_ref[...] / l_ref[...]
        outs = [jnp.sum(jnp.where(own, o[r * N_HEADS:(r + 1) * N_HEADS], 0.0), axis=0, keepdims=True)
                for r in range(SAMPLE_Q)]
        o_ref[0] = jnp.concatenate(outs + [jnp.zeros((SAMPLE_ROWS - SAMPLE_Q, ATT_W), F32)], axis=0)


def _sample_attn_t(pt_flat, q_pad, keep, knew_pad, vnew_pad, cache_kt, cache_vt, layer, n_pages):
    db = q_pad.shape[0]
    step_w = PAGES_PER_STEP * PAGE_SIZE
    grid_spec = pltpu.PrefetchScalarGridSpec(
        num_scalar_prefetch=1,
        grid=(db, n_pages // PAGES_PER_STEP),
        in_specs=[pl.BlockSpec((1, SAMPLE_ROWS, ATT_W), lambda b, j, pt: (b, 0, 0)),
                  pl.BlockSpec((1, SAMPLE_ROWS, step_w), lambda b, j, pt: (b, 0, j)),
                  pl.BlockSpec((1, SAMPLE_ROWS, PAGE_SIZE), lambda b, j, pt: (b, 0, n_pages)),
                  pl.BlockSpec((1, PAGE_SIZE, ATT_W), lambda b, j, pt: (b, 0, 0)),
                  pl.BlockSpec((1, PAGE_SIZE, ATT_W), lambda b, j, pt: (b, 0, 0))]
                 + 2 * _page_specs(layer, n_pages, (N_HEADS, HEAD_DIM, PAGE_SIZE)),
        out_specs=pl.BlockSpec((1, SAMPLE_ROWS, ATT_W), lambda b, j, pt: (b, 0, 0)),
        scratch_shapes=[pltpu.VMEM((SAMPLE_LINES, 1), F32), pltpu.VMEM((SAMPLE_LINES, 1), F32),
                        pltpu.VMEM((SAMPLE_LINES, ATT_W), F32)],
    )
    return pl.pallas_call(
        _sample_attn_kernel_t,
        grid_spec=grid_spec,
        out_shape=jax.ShapeDtypeStruct((db, SAMPLE_ROWS, ATT_W), F32),
        compiler_params=_cparams(("arbitrary", "arbitrary"), 32),
        name="sample_attn",
    )(pt_flat, q_pad, keep, keep, knew_pad, vnew_pad, *([cache_kt] * PAGES_PER_STEP), *([cache_vt] * PAGES_PER_STEP))


def _cmul(ar, ai, br, bi):
    return ar * br - ai * bi, ar * bi + ai * br


def _shift_rows(x, s, row, fill):
    return jnp.where(row >= s, pltpu.roll(x, s, 0), fill)


def _s5_kernel(u_ref, h0r_ref, h0i_ref, lam_ref, wb_ref, wc_ref, d_ref, wg_ref, bg_ref,
               z_ref, sr_ref, si_ref, hs_ref, car_ref, *, last_row):
    c = pl.program_id(1)
    chunk = u_ref.shape[1]
    n = SSM_N

    @pl.when(c == 0)
    def _():
        car_ref[:, 0:n] = h0r_ref[0]
        car_ref[:, n:2 * n] = h0i_ref[0]

    lr = lam_ref[0:1, :]
    li = lam_ref[1:2, :]
    dt = jnp.exp(lam_ref[2:3, :])
    mag = jnp.exp(lr * dt)
    p1r = mag * jnp.cos(li * dt)
    p1i = mag * jnp.sin(li * dt)
    den = lr * lr + li * li
    fr = ((p1r - 1.0) * lr + p1i * li) / den
    fi = (p1i * lr - (p1r - 1.0) * li) / den
    p2r, p2i = _cmul(p1r, p1i, p1r, p1i)
    p4r, p4i = _cmul(p2r, p2i, p2r, p2i)
    p8r, p8i = _cmul(p4r, p4i, p4r, p4i)
    row = lax.broadcasted_iota(jnp.int32, (SUBLANES, 1), 0)
    pwr = jnp.ones((SUBLANES, n), F32)
    pwi = jnp.zeros((SUBLANES, n), F32)
    for bit, (qr, qi) in ((1, (p1r, p1i)), (2, (p2r, p2i)), (4, (p4r, p4i)), (8, (p8r, p8i))):
        nr, ni = _cmul(pwr, pwi, qr, qi)
        take = ((row + 1) & bit) != 0
        pwr = jnp.where(take, nr, pwr)
        pwi = jnp.where(take, ni, pwi)

    u = u_ref[0]
    hs_ref[...] = jnp.dot(u.astype(BF16), wb_ref[...], preferred_element_type=F32)

    def group(g, carry):
        cr, ci = carry
        r0 = pl.multiple_of(g * SUBLANES, SUBLANES)
        gr = hs_ref[pl.ds(r0, SUBLANES), 0:n]
        gi = hs_ref[pl.ds(r0, SUBLANES), n:2 * n]
        xr, xi = _cmul(fr, fi, gr, gi)
        for s, (qr, qi) in ((1, (p1r, p1i)), (2, (p2r, p2i)), (4, (p4r, p4i))):
            sr, si = _cmul(qr, qi, _shift_rows(xr, s, row, 0.0), _shift_rows(xi, s, row, 0.0))
            xr, xi = xr + sr, xi + si
        tr, ti = _cmul(pwr, pwi, cr, ci)
        hr, hi = xr + tr, xi + ti
        hs_ref[pl.ds(r0, SUBLANES), 0:n] = hr
        hs_ref[pl.ds(r0, SUBLANES), n:2 * n] = hi
        return hr[SUBLANES - 1:SUBLANES], hi[SUBLANES - 1:SUBLANES]

    cr, ci = lax.fori_loop(0, chunk // SUBLANES, group, (car_ref[:, 0:n], car_ref[:, n:2 * n]))
    car_ref[:, 0:n] = cr
    car_ref[:, n:2 * n] = ci

    y = jnp.dot(hs_ref[...].astype(BF16), wc_ref[...], preferred_element_type=F32) + d_ref[...] * u
    z = jax.nn.gelu(y)
    gate = jax.nn.sigmoid(jnp.dot(z.astype(BF16), wg_ref[...], preferred_element_type=F32) + bg_ref[...])
    z_ref[0] = z * gate

    @pl.when(c == pl.num_programs(1) - 1)
    def _():
        sr_ref[0] = hs_ref[last_row:last_row + 1, 0:n]
        si_ref[0] = hs_ref[last_row:last_row + 1, n:2 * n]


def _blockdiag(blocks):
    g, r, c = blocks.shape
    eye = jnp.eye(g, dtype=blocks.dtype)
    return (blocks[:, :, None, :] * eye[:, None, :, None]).reshape(g * r, g * c)


def _s5(u3, h0r, h0i, t_valid, lam_re, lam_im, log_dt, b_re, b_im, c_re, c_im, d_skip, w_glu, b_glu):
    b, t, _ = u3.shape
    chunk = min(256, t)
    last_row = (t_valid - 1) % chunk
    lam = jnp.concatenate([lam_re.reshape(1, SSM_N), lam_im.reshape(1, SSM_N),
                           jnp.broadcast_to(log_dt[:, None], (SSM_GROUPS, SSM_STATE)).reshape(1, SSM_N),
                           jnp.zeros((SUBLANES - 3, SSM_N), F32)], axis=0)
    wb = jnp.concatenate([_blockdiag(jnp.swapaxes(b_re, 1, 2)), _blockdiag(jnp.swapaxes(b_im, 1, 2))],
                         axis=1).astype(BF16)
    wc = jnp.concatenate([_blockdiag(jnp.swapaxes(c_re, 1, 2)), -_blockdiag(jnp.swapaxes(c_im, 1, 2))],
                         axis=0).astype(BF16)
    full = lambda a: pl.BlockSpec(a.shape, lambda bi, c: (0,) * a.ndim)
    seq = pl.BlockSpec((1, chunk, SSM_W), lambda bi, c: (bi, c, 0))
    st = pl.BlockSpec((1, 1, SSM_N), lambda bi, c: (bi, 0, 0))
    d2 = d_skip.reshape(1, SSM_W)
    wg = w_glu.astype(BF16)
    bg = b_glu.reshape(1, SSM_W)
    return pl.pallas_call(
        functools.partial(_s5_kernel, last_row=last_row),
        grid=(b, t // chunk),
        in_specs=[seq, st, st, full(lam), full(wb), full(wc), full(d2), full(wg), full(bg)],
        out_specs=(seq, st, st),
        out_shape=(jax.ShapeDtypeStruct((b, t, SSM_W), F32), jax.ShapeDtypeStruct((b, 1, SSM_N), F32),
                   jax.ShapeDtypeStruct((b, 1, SSM_N), F32)),
        scratch_shapes=[pltpu.VMEM((chunk, 2 * SSM_N), F32), pltpu.VMEM((1, 2 * SSM_N), F32)],
        compiler_params=_cparams(("arbitrary", "arbitrary"), 48),
        name="s5",
    )(u3, h0r.reshape(b, 1, SSM_N), h0i.reshape(b, 1, SSM_N), lam, wb, wc, d2, wg, bg)


def _lru_kernel(x_ref, xg_ref, buf_ref, h0_ref, cw_ref, vec_ref, wa_ref, wx_ref,
                y_ref, nb_ref, hl_ref, ext_ref, a_ref, b_ref, car_ref, *, last_row):
    c = pl.program_id(1)
    chunk = x_ref.shape[1]
    pad = SUBLANES

    @pl.when(c == 0)
    def _():
        ext_ref[0:pad, :] = jnp.zeros((pad, LRU_W), F32)
        ext_ref[pad - (CONV_W - 1):pad, :] = buf_ref[0]
        car_ref[...] = h0_ref[0]

    @pl.when(c > 0)
    def _():
        ext_ref[0:pad, :] = ext_ref[chunk:chunk + pad, :]

    ext_ref[pad:pad + chunk, :] = x_ref[0]
    xc = vec_ref[0:1, :] + cw_ref[CONV_W - 1:CONV_W, :] * x_ref[0]
    for jj in range(CONV_W - 1):
        xc = xc + cw_ref[jj:jj + 1, :] * ext_ref[pad - (CONV_W - 1) + jj:pad - (CONV_W - 1) + jj + chunk, :]
    xb = xc.astype(BF16)
    r = jax.nn.sigmoid(jnp.dot(xb, wa_ref[...], preferred_element_type=F32) + vec_ref[1:2, :])
    gi = jax.nn.sigmoid(jnp.dot(xb, wx_ref[...], preferred_element_type=F32) + vec_ref[2:3, :])
    lam = vec_ref[3:4, :]
    log_sig = -(jnp.maximum(-lam, 0.0) + jnp.log(1.0 + jnp.exp(-jnp.abs(lam))))
    log_a = LRU_C * r * log_sig
    a = jnp.exp(log_a)
    one_minus_a2 = -jnp.tanh(log_a) * (jnp.exp(2.0 * log_a) + 1.0)
    a_ref[...] = a
    b_ref[...] = jnp.sqrt(one_minus_a2) * (gi * xc)
    row = lax.broadcasted_iota(jnp.int32, (SUBLANES, 1), 0)

    def group(g, carry):
        r0 = pl.multiple_of(g * SUBLANES, SUBLANES)
        av = a_ref[pl.ds(r0, SUBLANES), :]
        bv = b_ref[pl.ds(r0, SUBLANES), :]
        for s in (1, 2, 4):
            bv = av * _shift_rows(bv, s, row, 0.0) + bv
            av = av * _shift_rows(av, s, row, 1.0)
        h = bv + av * carry
        b_ref[pl.ds(r0, SUBLANES), :] = h
        return h[SUBLANES - 1:SUBLANES]

    car_ref[...] = lax.fori_loop(0, chunk // SUBLANES, group, car_ref[...])
    y_ref[0] = b_ref[...] * jax.nn.gelu(xg_ref[0])

    @pl.when(c == pl.num_programs(1) - 1)
    def _():
        hl_ref[0] = b_ref[last_row:last_row + 1, :]
        lo = pad + last_row - (CONV_W - 2)
        nb_ref[0] = ext_ref[lo:lo + CONV_W - 1, :]


def _rglru(x3, xg3, conv_buf, h0, t_valid, conv_w, conv_b, w_a, b_a, w_x, b_x, lam):
    b, t, _ = x3.shape
    chunk = min(256, t)
    last_row = (t_valid - 1) % chunk
    vec = jnp.concatenate([conv_b[None], b_a[None], b_x[None], lam[None], jnp.zeros((SUBLANES - 4, LRU_W), F32)], 0)
    wa = _blockdiag(w_a).astype(BF16)
    wx = _blockdiag(w_x).astype(BF16)
    full = lambda a: pl.BlockSpec(a.shape, lambda bi, c: (0,) * a.ndim)
    seq = pl.BlockSpec((1, chunk, LRU_W), lambda bi, c: (bi, c, 0))
    return pl.pallas_call(
        functools.partial(_lru_kernel, last_row=last_row),
        grid=(b, t // chunk),
        in_specs=[seq, seq, pl.BlockSpec((1, CONV_W - 1, LRU_W), lambda bi, c: (bi, 0, 0)),
                  pl.BlockSpec((1, 1, LRU_W), lambda bi, c: (bi, 0, 0)), full(conv_w), full(vec), full(wa), full(wx)],
        out_specs=(seq, pl.BlockSpec((1, CONV_W - 1, LRU_W), lambda bi, c: (bi, 0, 0)),
                   pl.BlockSpec((1, 1, LRU_W), lambda bi, c: (bi, 0, 0))),
        out_shape=(jax.ShapeDtypeStruct((b, t, LRU_W), F32), jax.ShapeDtypeStruct((b, CONV_W - 1, LRU_W), F32),
                   jax.ShapeDtypeStruct((b, 1, LRU_W), F32)),
        scratch_shapes=[pltpu.VMEM((chunk + 2 * SUBLANES, LRU_W), F32), pltpu.VMEM((chunk, LRU_W), F32),
                        pltpu.VMEM((chunk, LRU_W), F32), pltpu.VMEM((1, LRU_W), F32)],
        compiler_params=_cparams(("arbitrary", "arbitrary"), 32),
        name="rglru",
    )(x3, xg3, conv_buf, h0.reshape(b, 1, LRU_W), conv_w, vec, wa, wx)


def _layer_norm(x, g, b):
    mu = jnp.mean(x, axis=-1, keepdims=True)
    xc = x - mu
    var = jnp.mean(xc * xc, axis=-1, keepdims=True)
    return xc * lax.rsqrt(var + LN_EPS) * g + b


def _mix_kernel(ya_ref, ys_ref, yl_ref, gl_ref, x_ref, wbr_ref, wout_ref, vec_ref, wr_ref, br_ref,
                x1_ref, x1b_ref, comb_ref, pos_ref, cnt_ref, *, alpha, tiles_per_block):
    i = pl.program_id(0)
    tm = x_ref.shape[0]
    mixed = None
    for n, y_ref in enumerate((ya_ref, ys_ref, yl_ref)):
        proj = jnp.dot(y_ref[...].astype(BF16), wbr_ref[n], preferred_element_type=F32)
        t = jax.nn.sigmoid(gl_ref[:, n * D_MODEL:(n + 1) * D_MODEL]) * proj
        mixed = t if mixed is None else mixed + t
    mixed = jnp.dot(mixed.astype(BF16), wout_ref[...], preferred_element_type=F32) + vec_ref[0:1, :]
    x1 = _layer_norm(alpha * x_ref[...] + mixed, vec_ref[1:2, :], vec_ref[2:3, :])
    x1_ref[...] = x1
    x1b_ref[...] = x1.astype(BF16)

    logits = jnp.dot(x1.astype(BF16), wr_ref[...].astype(BF16), preferred_element_type=F32) + br_ref[...]
    lane = lax.broadcasted_iota(jnp.int32, (tm, N_EXPERTS), 1)
    work = logits
    vals, hits = [], []
    for _ in range(TOP_K):
        v = jnp.max(work, axis=1, keepdims=True)
        ix = jnp.min(jnp.where(work == v, lane, N_EXPERTS), axis=1, keepdims=True)
        hit = lane == ix
        vals.append(v)
        hits.append(hit)
        work = jnp.where(hit, -jnp.inf, work)
    es = [jnp.exp(v - vals[0]) for v in vals]
    den = es[0] + es[1] + es[2] + es[3]
    comb = jnp.zeros((tm, N_EXPERTS), F32)
    routed = jnp.zeros((tm, N_EXPERTS), F32)
    for e, hit in zip(es, hits):
        comb = comb + jnp.where(hit, e / den, 0.0)
        routed = routed + jnp.where(hit, 1.0, 0.0)
    comb_ref[...] = comb

    @pl.when(i % tiles_per_block == 0)
    def _():
        cnt_ref[...] = jnp.zeros(cnt_ref.shape, F32)

    before = (lax.broadcasted_iota(jnp.int32, (tm, tm), 1) < lax.broadcasted_iota(jnp.int32, (tm, tm), 0))
    rank = jnp.dot(jnp.where(before, 1.0, 0.0).astype(BF16), routed.astype(BF16), preferred_element_type=F32)
    pos = rank + cnt_ref[...]
    pos_ref[...] = jnp.where(routed > 0.5, pos, -1.0)
    cnt_ref[...] = cnt_ref[...] + jnp.sum(routed, axis=0, keepdims=True)


def _mix(ya, ys, yl, gl, x, w_branch, w_out, b_out, ln_g, ln_b, w_r, b_r, alpha, moe_block):
    n = x.shape[0]
    tm = min(256, n)
    row = lambda wd: pl.BlockSpec((tm, wd), lambda i: (i, 0))
    full = lambda a: pl.BlockSpec(a.shape, lambda i: (0,) * a.ndim)
    vec = jnp.concatenate([b_out[None], ln_g[None], ln_b[None], jnp.zeros((SUBLANES - 3, D_MODEL), F32)], 0)
    wbr = w_branch.astype(BF16)
    wout = w_out.astype(BF16)
    br = b_r.reshape(1, N_EXPERTS)
    return pl.pallas_call(
        functools.partial(_mix_kernel, alpha=alpha, tiles_per_block=moe_block // tm),
        grid=(n // tm,),
        in_specs=[row(ATT_W), row(SSM_W), row(LRU_W), row(N_BRANCH * D_MODEL), row(D_MODEL),
                  full(wbr), full(wout), full(vec), full(w_r), full(br)],
        out_specs=(row(D_MODEL), row(D_MODEL), row(N_EXPERTS), row(N_EXPERTS)),
        out_shape=(jax.ShapeDtypeStruct((n, D_MODEL), F32), jax.ShapeDtypeStruct((n, D_MODEL), BF16),
                   jax.ShapeDtypeStruct((n, N_EXPERTS), F32), jax.ShapeDtypeStruct((n, N_EXPERTS), F32)),
        scratch_shapes=[pltpu.VMEM((1, N_EXPERTS), F32)],
        compiler_params=_cparams(("arbitrary",), 48),
        name="mix_ln1_router",
    )(ya, ys, yl, gl, x, wbr, wout, vec, w_r, br)


MOE_BLOCK = 2048
MOE_CHUNK = 256
MOE_TAIL = 64
MOE_SCATTER = 512


def _moe_kernel(cnt_ref, xb_ref, post_ref, pos_ref, comb_ref, wgu_ref, bgu_ref, wdn_ref, bdn_ref, o_ref,
                *, ch, ch_tail, sb):
    tb_i = pl.program_id(0)
    e = pl.program_id(1)
    tb = xb_ref.shape[0]

    @pl.when(e == 0)
    def _():
        o_ref[...] = jnp.zeros(o_ref.shape, F32)

    cnt = cnt_ref[tb_i * N_EXPERTS + e]
    full = cnt // ch
    rem = cnt % ch
    short = jnp.logical_and(rem > 0, rem <= ch_tail)
    widen = jnp.logical_and(short, full > 0)
    tail_only = jnp.logical_and(short, full == 0)
    n_main = full + jnp.where(jnp.logical_and(rem > 0, jnp.logical_not(short)), 1, 0) - jnp.where(widen, 1, 0)
    lane_e = lax.broadcasted_iota(jnp.int32, (1, N_EXPERTS), 1) == e
    slot_row = post_ref[0]

    def run_chunk(base, ch):
        pick = (lax.broadcasted_iota(jnp.int32, (ch, tb), 0) + base).astype(F32) == slot_row
        xc = jnp.dot(jnp.where(pick, 1.0, 0.0).astype(BF16), xb_ref[...], preferred_element_type=F32).astype(BF16)
        gu = jnp.dot(xc, wgu_ref[0], preferred_element_type=F32) + bgu_ref[0]
        gate = jnp.minimum(gu[:, :D_FF], SWIGLU_LIMIT)
        up = jnp.clip(gu[:, D_FF:], -SWIGLU_LIMIT, SWIGLU_LIMIT)
        act = (up + 1.0) * gate * jax.nn.sigmoid(SWIGLU_ALPHA * gate)
        y = (jnp.dot(act.astype(BF16), wdn_ref[0], preferred_element_type=F32) + bdn_ref[0]).astype(BF16)
        for s in range(tb // sb):
            rows = slice(s * sb, (s + 1) * sb)
            slot_col = jnp.sum(jnp.where(lane_e, pos_ref[rows, :], 0.0), axis=1, keepdims=True)
            w_col = jnp.sum(jnp.where(lane_e, comb_ref[rows, :], 0.0), axis=1, keepdims=True)
            upd = None
            for k0 in range(0, ch, MOE_CHUNK):
                kw = min(MOE_CHUNK, ch - k0)
                put = (lax.broadcasted_iota(jnp.int32, (sb, kw), 1) + (base + k0)).astype(F32) == slot_col
                t = jnp.dot(jnp.where(put, w_col, 0.0).astype(BF16), y[k0:k0 + kw], preferred_element_type=F32)
                upd = t if upd is None else upd + t
            o_ref[rows, :] += upd

    def main_body(c, carry):
        run_chunk(c * ch, ch)
        return carry

    lax.fori_loop(0, n_main, main_body, 0)

    @pl.when(widen)
    def _():
        run_chunk(n_main * ch, ch + ch_tail)

    @pl.when(tail_only)
    def _():
        run_chunk(0, ch_tail)


def _moe(x1b, comb, pos, wgu, bgu, wdn, bdn):
    n = x1b.shape[0]
    tb = min(MOE_BLOCK, n)
    ch = min(MOE_CHUNK, tb)
    ch_tail = min(MOE_TAIL, ch)
    sb = min(MOE_SCATTER, tb)
    n_tb = n // tb
    routed = (pos >= 0).astype(jnp.int32)
    cnt = routed.reshape(n_tb, tb, N_EXPERTS).sum(axis=1).reshape(-1)
    pos_t = pos.T.reshape(N_EXPERTS, 1, n)
    grid_spec = pltpu.PrefetchScalarGridSpec(
        num_scalar_prefetch=1,
        grid=(n_tb, N_EXPERTS),
        in_specs=[pl.BlockSpec((tb, D_MODEL), lambda t, e, c: (t, 0)),
                  pl.BlockSpec((1, 1, tb), lambda t, e, c: (e, 0, t)),
                  pl.BlockSpec((tb, N_EXPERTS), lambda t, e, c: (t, 0)),
                  pl.BlockSpec((tb, N_EXPERTS), lambda t, e, c: (t, 0)),
                  pl.BlockSpec((1, D_MODEL, 2 * D_FF), lambda t, e, c: (e, 0, 0)),
                  pl.BlockSpec((1, 1, 2 * D_FF), lambda t, e, c: (e, 0, 0)),
                  pl.BlockSpec((1, D_FF, D_MODEL), lambda t, e, c: (e, 0, 0)),
                  pl.BlockSpec((1, 1, D_MODEL), lambda t, e, c: (e, 0, 0))],
        out_specs=pl.BlockSpec((tb, D_MODEL), lambda t, e, c: (t, 0)),
    )
    return pl.pallas_call(
        functools.partial(_moe_kernel, ch=ch, ch_tail=ch_tail, sb=sb),
        grid_spec=grid_spec,
        out_shape=jax.ShapeDtypeStruct((n, D_MODEL), F32),
        compiler_params=_cparams(("arbitrary", "arbitrary"), 56),
        name="moe",
    )(cnt, x1b, pos_t, pos, comb, wgu, bgu, wdn, bdn)


def _ln2_kernel(x_ref, m_ref, vec_ref, o_ref, *, alpha):
    o_ref[...] = _layer_norm(alpha * x_ref[...] + m_ref[...], vec_ref[0:1, :], vec_ref[1:2, :])


def _ln2(x1, moe_out, g, b, alpha):
    n = x1.shape[0]
    tm = min(512, n)
    vec = jnp.concatenate([g[None], b[None], jnp.zeros((SUBLANES - 2, D_MODEL), F32)], 0)
    row = pl.BlockSpec((tm, D_MODEL), lambda i: (i, 0))
    return pl.pallas_call(
        functools.partial(_ln2_kernel, alpha=alpha),
        grid=(n // tm,),
        in_specs=[row, row, pl.BlockSpec(vec.shape, lambda i: (0, 0))],
        out_specs=row,
        out_shape=jax.ShapeDtypeStruct((n, D_MODEL), F32),
        compiler_params=_cparams(("arbitrary",), 32),
        name="ln2",
    )(x1, moe_out, vec)


def _pad_rows(a, rows):
    pad = [(0, 0)] * a.ndim
    pad[1] = (0, rows - a.shape[1])
    return jnp.pad(a, pad)


def kernel(x_prompt, x_sample, cache_k, cache_v, cache_kidx, state_ssm_re, state_ssm_im, state_lru_h, state_lru_conv, page_table, w_in, b_in, ssm_lam_re, ssm_lam_im, ssm_log_dt, ssm_b_re, ssm_b_im, ssm_c_re, ssm_c_im, ssm_d, ssm_w_glu, ssm_b_glu, lru_conv_w, lru_conv_b, lru_w_a, lru_b_a, lru_w_x, lru_b_x, lru_lam, w_branch, w_out, b_out, ln1_g, ln1_b, moe_w_r, moe_b_r, moe_w_gu, moe_b_gu, moe_w_dn, moe_b_dn, ln2_g, ln2_b):
    depth = w_in.shape[0]
    alpha = (2 * depth) ** 0.25
    bp, tp, _ = x_prompt.shape
    db, ns, _ = x_sample.shape
    assert ns <= SAMPLE_Q
    n_pages = page_table.shape[1]
    past = n_pages * PAGE_SIZE
    pt_flat = page_table.reshape(-1).astype(jnp.int32)
    cache_kt = jnp.transpose(cache_k, (0, 1, 3, 4, 2))
    cache_vt = jnp.transpose(cache_v, (0, 1, 3, 4, 2))
    cache_kidx_t = jnp.transpose(cache_kidx, (0, 1, 3, 2))
    topk_p = min(TOPK_MAX, tp // 4)
    topk_s = min(TOPK_MAX, (past + ns) // 4)

    q_end = 3 * ATT_W
    i_end = q_end + IDX_KI + IDX_DIM + IDX_HEADS

    def regroup(a):
        padw = [(0, 0)] * (a.ndim - 1) + [(0, IDX_PAD - (i_end - q_end))]
        return jnp.concatenate([a[..., :q_end], jnp.pad(a[..., q_end:i_end], padw), a[..., i_end:]], axis=-1)

    w_in_r = regroup(w_in).astype(BF16)
    b_in_r = regroup(b_in).reshape(depth, 1, N_IN_PAD)
    wgu_b = moe_w_gu.astype(BF16)
    wdn_b = moe_w_dn.astype(BF16)
    bgu3 = moe_b_gu.reshape(depth, N_EXPERTS, 1, 2 * D_FF)
    bdn3 = moe_b_dn.reshape(depth, N_EXPERTS, 1, D_MODEL)

    xp = x_prompt.reshape(bp * tp, D_MODEL)
    xs = x_sample.reshape(db * ns, D_MODEL)
    new_p = [[] for _ in range(7)]
    new_s = [[] for _ in range(7)]

    def channel_mix(l, x, ya, ys, yl, gl):
        n = x.shape[0]
        x1, x1b, comb, pos = _mix(ya, ys, yl, gl, x, w_branch[l], w_out[l], b_out[l], ln1_g[l], ln1_b[l],
                                  moe_w_r[l], moe_b_r[l], alpha, min(MOE_BLOCK, n))
        moe_out = _moe(x1b, comb, pos, wgu_b[l], bgu3[l], wdn_b[l], bdn3[l])
        return _ln2(x1, moe_out, ln2_g[l], ln2_b[l], alpha)

    for l in range(depth):
        ssm_p = (ssm_lam_re[l], ssm_lam_im[l], ssm_log_dt[l], ssm_b_re[l], ssm_b_im[l], ssm_c_re[l],
                 ssm_c_im[l], ssm_d[l], ssm_w_glu[l], ssm_b_glu[l])
        lru_p = (lru_conv_w[l], lru_conv_b[l], lru_w_a[l], lru_b_a[l], lru_w_x[l], lru_b_x[l], lru_lam[l])

        q_hm, k, k_hm, v, vt_hm, idx, u, xr, xg, gl = _in_proj(xp, w_in_r[l], b_in_r[l])
        r3 = lambda a: a.reshape(bp, tp, a.shape[-1])
        idx3 = r3(idx)
        wi_t = _pad_rows(jnp.swapaxes(idx3[:, :, IDX_WI:IDX_WI + IDX_HEADS], 1, 2), SUBLANES)
        bias_t = _dsa_index_t(idx3, wi_t, topk_p)
        ya = jnp.swapaxes(_dsa_attn_t(q_hm, k_hm, vt_hm, bias_t), 1, 2).reshape(bp * tp, ATT_W)
        zeros_n = jnp.zeros((bp, SSM_N), F32)
        ys, s_re, s_im = _s5(r3(u), zeros_n, zeros_n, tp, *ssm_p)
        yl, nbuf, hl = _rglru(r3(xr), r3(xg), jnp.zeros((bp, CONV_W - 1, LRU_W), F32), jnp.zeros((bp, LRU_W), F32),
                              tp, *lru_p)
        st = (k.reshape(bp, tp, N_HEADS, HEAD_DIM), v.reshape(bp, tp, N_HEADS, HEAD_DIM),
              idx3[:, :, IDX_KI:IDX_KI + IDX_DIM], s_re.reshape(bp, SSM_GROUPS, SSM_STATE),
              s_im.reshape(bp, SSM_GROUPS, SSM_STATE), nbuf, hl.reshape(bp, LRU_W))
        for jj in range(7):
            new_p[jj].append(st[jj])
        xp = channel_mix(l, xp, ya, ys.reshape(bp * tp, SSM_W), yl.reshape(bp * tp, LRU_W), gl)

        q_hm, k, _, v, _, idx, u, xr, xg, gl = _in_proj(xs, w_in_r[l], b_in_r[l])
        r3 = lambda a: a.reshape(db, ns, a.shape[-1])
        idx3 = r3(idx)
        ki_new = idx3[:, :, IDX_KI:IDX_KI + IDX_DIM]
        keep = _sample_index(pt_flat, _pad_rows(idx3, SAMPLE_ROWS), _pad_rows(ki_new, PAGE_SIZE), cache_kidx_t, l,
                             n_pages, topk_s, ns)
        q_rows = jnp.transpose(q_hm.reshape(N_HEADS, db, ns, HEAD_DIM), (1, 2, 0, 3)).reshape(db, ns, ATT_W)
        ya = _sample_attn_t(pt_flat, _pad_rows(q_rows, SAMPLE_ROWS), keep, _pad_rows(r3(k).astype(BF16), PAGE_SIZE),
                            _pad_rows(r3(v).astype(BF16), PAGE_SIZE), cache_kt, cache_vt, l, n_pages)
        ya = ya[:, :ns].reshape(db * ns, ATT_W)
        ys, s_re, s_im = _s5(_pad_rows(r3(u), SAMPLE_ROWS), state_ssm_re[l].reshape(db, SSM_N),
                             state_ssm_im[l].reshape(db, SSM_N), ns, *ssm_p)
        yl, nbuf, hl = _rglru(_pad_rows(r3(xr), SAMPLE_ROWS), _pad_rows(r3(xg), SAMPLE_ROWS), state_lru_conv[l],
                              state_lru_h[l], ns, *lru_p)
        st = (k.reshape(db, ns, N_HEADS, HEAD_DIM), v.reshape(db, ns, N_HEADS, HEAD_DIM), ki_new,
              s_re.reshape(db, SSM_GROUPS, SSM_STATE), s_im.reshape(db, SSM_GROUPS, SSM_STATE), nbuf,
              hl.reshape(db, LRU_W))
        for jj in range(7):
            new_s[jj].append(st[jj])
        xs = channel_mix(l, xs, ya, ys[:, :ns].reshape(db * ns, SSM_W), yl[:, :ns].reshape(db * ns, LRU_W), gl)

    k_p, v_p, ki_p, sre_p, sim_p, lc_p, lh_p = [jnp.stack(a) for a in new_p]
    k_s, v_s, ki_s, sre_s, sim_s, lc_s, lh_s = [jnp.stack(a) for a in new_s]
    return (xp.reshape(bp, tp, D_MODEL), xs.reshape(db, ns, D_MODEL), k_p, v_p, ki_p, sre_p, sim_p, lh_p, lc_p,
            k_s, v_s, ki_s, sre_s, sim_s, lh_s, lc_s)
```
